```python
import math
import functools
import jax
import jax.numpy as jnp
from jax import lax
import numpy as np

D_MODEL = 2048
BATCH = 32
SEQ = 256
DEPTH = 2
DEC_BATCH = 2
DEC_SEQ = 4096
PAST_LEN = 256

GRID_W = 64
N_BRANCH = 4
BR_W = D_MODEL // 4
CONV_W = 4
RG_W = BR_W
RG_HEADS = 4
RG_HD = RG_W // RG_HEADS
RG_C = 8.0
POOL_WINDOWS = (2, 4, 8, 16)
POOL_G = BR_W // len(POOL_WINDOWS)
SGU_W = BR_W
SGU_HEADS = 4
SGU_HD = SGU_W // SGU_HEADS
SGU_CHUNK = 128
DN_DK = 128
DN_DV = 128
DN_HEADS = BR_W // DN_DV
DN_W = DN_HEADS * DN_DV
DN_QKV = DN_HEADS * (2 * DN_DK + DN_DV)
DN_CHUNK = 64
OFF_RG_X = 0
OFF_RG_G = OFF_RG_X + RG_W
OFF_POOL = OFF_RG_G + RG_W
OFF_SGU = OFF_POOL + BR_W
OFF_DN_QKV = OFF_SGU + 2 * SGU_W
OFF_DN_Z = OFF_DN_QKV + DN_QKV
OFF_DN_BA = OFF_DN_Z + DN_W
OFF_GATE = OFF_DN_BA + 2 * 2 * DN_HEADS
N_IN = OFF_GATE + N_BRANCH * D_MODEL
D_FF = 11 * D_MODEL // 4
N_EXPERTS = 8
TOP_K = 2
D_FF_EXPERT = D_FF // 2
N_DENSE = (DEPTH + 1) // 2
N_MOE = DEPTH // 2
EPS = 1e-6
F32 = jnp.float32

kernel_name = 'hybrid_flow_rglru_pool_gmlp_deltanet'


def rms_norm(x, g):
    xf = x.astype(F32)
    y = xf * lax.rsqrt(jnp.mean(xf * xf, axis=-1, keepdims=True) + EPS)
    return (y * g.astype(F32)).astype(x.dtype)


def causal_dwconv(x, w):
    L = x.shape[1]
    xp = jnp.pad(x, ((0, 0), (CONV_W - 1, 0), (0, 0)))
    return sum(xp[:, j:j + L] * w[j] for j in range(CONV_W))


def linear_scan(a, b, h0):
    b = b.at[:, 0].add(a[:, 0] * h0)

    def combine(left, right):
        return left[0] * right[0], right[0] * left[1] + right[1]

    _, h = lax.associative_scan(combine, (a, b), axis=1)
    return h


def rglru_direction(x, conv_w, conv_b, wa, ba, wx, bx, lam, h0):
    bsz, L, _ = x.shape
    xc = causal_dwconv(x, conv_w) + conv_b
    xh = xc.reshape(bsz, L, RG_HEADS, RG_HD)
    r = jax.nn.sigmoid(jnp.einsum('blhi,hij->blhj', xh, wa).reshape(bsz, L, RG_W).astype(F32) + ba.astype(F32))
    i = jax.nn.sigmoid(jnp.einsum('blhi,hij->blhj', xh, wx).reshape(bsz, L, RG_W).astype(F32) + bx.astype(F32))
    log_a = -RG_C * jax.nn.softplus(-lam.astype(F32)) * r
    a = jnp.exp(log_a)
    b = jnp.sqrt(-jnp.expm1(2.0 * log_a)) * i * xc.astype(F32)
    h = linear_scan(a, b, h0.astype(F32))
    return h, h[:, -1]


def rglru_mixer(xr, xg, conv_w, conv_b, wa, ba, wx, bx, lam, h0):
    hf, sf = rglru_direction(xr, conv_w[0], conv_b[0], wa[0], ba[0], wx[0], bx[0], lam[0], h0[:, 0])
    hb, sb = rglru_direction(xr[:, ::-1], conv_w[1], conv_b[1], wa[1], ba[1], wx[1], bx[1], lam[1], h0[:, 1])
    y = jax.nn.gelu(xg.astype(F32)) * (hf + hb[:, ::-1])
    return y.astype(xr.dtype), jnp.stack([sf, sb], axis=1)


def pool_mixer(x, w_pool, scale):
    n = x.shape[-2]
    xf = x.astype(F32)
    cs = jnp.cumsum(xf, axis=-2)
    cs = jnp.concatenate([jnp.zeros_like(cs[..., :1, :]), cs], axis=-2)
    t = jnp.arange(n)
    outs = []
    for j, w in enumerate(POOL_WINDOWS):
        lo = jnp.maximum(t - w // 2, 0)
        hi = jnp.minimum(t + (w - 1 - w // 2), n - 1)
        csj = cs[..., j * POOL_G:(j + 1) * POOL_G]
        s = jnp.take(csj, hi + 1, axis=-2) - jnp.take(csj, lo, axis=-2)
        cnt = (hi - lo + 1).astype(F32)[:, None]
        outs.append(s / cnt - xf[..., j * POOL_G:(j + 1) * POOL_G])
    pooled = jnp.stack(outs, axis=-2)
    y = jnp.einsum('...ngc,gcd->...ngd', pooled, w_pool.astype(F32))
    return (y.reshape(x.shape) * scale.astype(F32)).astype(x.dtype)


def sgu_mixer(z, ln_g, ln_b, ws, bs):
    bsz, L, _ = z.shape
    u = z[..., :SGU_W].astype(F32)
    v = z[..., SGU_W:].astype(F32)
    mu = jnp.mean(v, axis=-1, keepdims=True)
    var = jnp.mean(jnp.square(v - mu), axis=-1, keepdims=True)
    v = (v - mu) * lax.rsqrt(var + EPS) * ln_g.astype(F32) + ln_b.astype(F32)
    vc = v.reshape(bsz, L // SGU_CHUNK, SGU_CHUNK, SGU_HEADS, SGU_HD)
    s = jnp.einsum('bnphc,hqp->bnqhc', vc, ws.astype(F32)) + bs.astype(F32).T[:, :, None]
    return (u * s.reshape(bsz, L, SGU_W)).astype(z.dtype)


def l2norm(x):
    return x * lax.rsqrt(jnp.sum(x * x, axis=-1, keepdims=True) + EPS)


def chunk_gated_delta(q, k, v, g, beta, s0):
    bsz, L, H, _ = q.shape
    V = v.shape[-1]
    C = DN_CHUNK
    N = L // C

    def chunks(t):
        return jnp.moveaxis(t.reshape(bsz, N, C, H, *t.shape[3:]), 3, 1)

    q, k, v, beta = chunks(q), chunks(k), chunks(v), chunks(beta)
    g = jnp.cumsum(chunks(g), axis=-1)
    incl = jnp.tril(jnp.ones((C, C), bool))
    strict = jnp.tril(jnp.ones((C, C), bool), -1)
    decay = jnp.exp(jnp.where(incl, g[..., :, None] - g[..., None, :], -jnp.inf))
    kb = k * beta[..., None]
    a_mat = jnp.where(strict, jnp.einsum('bhnck,bhndk->bhncd', kb, k) * decay, 0.0) + jnp.eye(C, dtype=F32)
    rhs = jnp.concatenate([v * beta[..., None], kb * jnp.exp(g)[..., None]], axis=-1)
    w = lax.linalg.triangular_solve(a_mat, rhs, left_side=True, lower=True, unit_diagonal=True)
    val, kcum = w[..., :V], w[..., V:]
    qk = jnp.einsum('bhnck,bhndk->bhncd', q, k) * decay
    qg = q * jnp.exp(g)[..., None]
    kend = k * jnp.exp(g[..., -1:] - g)[..., None]
    gend = jnp.exp(g[..., -1])

    def step(s, xs):
        qk_i, qg_i, kc_i, val_i, ke_i, ge_i = xs
        u = val_i - jnp.einsum('bhck,bhkv->bhcv', kc_i, s)
        o = jnp.einsum('bhck,bhkv->bhcv', qg_i, s) + jnp.einsum('bhcd,bhdv->bhcv', qk_i, u)
        s = s * ge_i[..., None, None] + jnp.einsum('bhck,bhcv->bhkv', ke_i, u)
        return s, o

    xs = tuple(jnp.moveaxis(t, 2, 0) for t in (qk, qg, kcum, val, kend, gend))
    s_fin, o = lax.scan(step, s0, xs)
    o = jnp.transpose(o, (1, 0, 3, 2, 4)).reshape(bsz, L, H, V)
    return o, s_fin


def dn_direction(qkv, b_logit, a_logit, conv_w, a_log, dt_bias, s0):
    bsz, L, _ = qkv.shape
    hk = DN_HEADS * DN_DK
    y = jax.nn.silu(causal_dwconv(qkv, conv_w)).astype(F32)
    q = l2norm(y[..., :hk].reshape(bsz, L, DN_HEADS, DN_DK)) * (DN_DK ** -0.5)
    k = l2norm(y[..., hk:2 * hk].reshape(bsz, L, DN_HEADS, DN_DK))
    v = y[..., 2 * hk:].reshape(bsz, L, DN_HEADS, DN_DV)
    beta = jax.nn.sigmoid(b_logit.astype(F32))
    g = -jnp.exp(a_log.astype(F32)) * jax.nn.softplus(a_logit.astype(F32) + dt_bias.astype(F32))
    return chunk_gated_delta(q, k, v, g, beta, s0.astype(F32))


def dn_mixer(qkv, z, ba, conv_w, a_log, dt_bias, norm_g, s0):
    bsz, L, _ = qkv.shape
    ba = ba.reshape(bsz, L, 2, 2, DN_HEADS)
    o_f, s_f = dn_direction(qkv, ba[:, :, 0, 0], ba[:, :, 0, 1], conv_w[0], a_log[0], dt_bias[0], s0[:, 0])
    rev = ba[:, ::-1]
    o_b, s_b = dn_direction(qkv[:, ::-1], rev[:, :, 1, 0], rev[:, :, 1, 1], conv_w[1], a_log[1], dt_bias[1], s0[:, 1])
    o = o_f + o_b[:, ::-1]
    o = o * lax.rsqrt(jnp.mean(o * o, axis=-1, keepdims=True) + EPS) * norm_g.astype(F32)
    o = o.reshape(bsz, L, DN_W) * jax.nn.silu(z.astype(F32))
    return o.astype(qkv.dtype), jnp.stack([s_f, s_b], axis=1)


def mixer_block(h, lp, rg_h0, dn_s0, grid):
    bsz, L, _ = h.shape
    proj = jnp.einsum('bld,dn->bln', h, lp['w_in'])
    y_a, rg_state = rglru_mixer(proj[..., OFF_RG_X:OFF_RG_X + RG_W], proj[..., OFF_RG_G:OFF_RG_G + RG_W],
                                lp['rg_conv_w'], lp['rg_conv_b'], lp['rg_wa'], lp['rg_ba'],
                                lp['rg_wx'], lp['rg_bx'], lp['rg_lam'], rg_h0)
    pin = proj[..., OFF_POOL:OFF_POOL + BR_W]
    if grid:
        rows = L // GRID_W
        y_b = pool_mixer(pin.reshape(bsz, rows, GRID_W, BR_W), lp['pool_w'], lp['pool_scale']).reshape(bsz, L, BR_W)
    else:
        y_b = pool_mixer(pin, lp['pool_w'], lp['pool_scale'])
    y_c = sgu_mixer(jax.nn.gelu(proj[..., OFF_SGU:OFF_SGU + 2 * SGU_W]),
                    lp['sgu_ln_g'], lp['sgu_ln_b'], lp['sgu_ws'], lp['sgu_bs'])
    y_d, dn_state = dn_mixer(proj[..., OFF_DN_QKV:OFF_DN_QKV + DN_QKV], proj[..., OFF_DN_Z:OFF_DN_Z + DN_W],
                             proj[..., OFF_DN_BA:OFF_GATE], lp['dn_conv_w'], lp['dn_a_log'],
                             lp['dn_dt_bias'], lp['dn_norm_g'], dn_s0)
    ys = jnp.stack([y_a, y_b, y_c, y_d], axis=2)
    br = jnp.einsum('blkc,kcd->blkd', ys, lp['w_br'])
    gates = jax.nn.sigmoid(proj[..., OFF_GATE:].reshape(bsz, L, N_BRANCH, D_MODEL))
    merged = jnp.sum(gates * br, axis=2)
    return merged @ lp['w_out'], rg_state, dn_state


def swiglu(x, w1, w3, w2):
    return (jax.nn.silu(x @ w1) * (x @ w3)) @ w2


def moe_swiglu(x, wr, br, w1, w3, w2):
    shp = x.shape
    xt = x.reshape(-1, shp[-1])
    logits = (xt @ wr).astype(F32) + br.astype(F32)
    top_v, top_i = lax.top_k(logits, TOP_K)
    wts = jax.nn.softmax(top_v, axis=-1)
    gate = jnp.sum(jax.nn.one_hot(top_i, N_EXPERTS, dtype=F32) * wts[..., None], axis=1)
    out = jnp.zeros(xt.shape, F32)
    for e in range(N_EXPERTS):
        out = out + gate[:, e:e + 1] * swiglu(xt, w1[e], w3[e], w2[e]).astype(F32)
    return out.astype(x.dtype).reshape(shp)


def trunk_layer(x, mod, lp, ffn, rg_h0, dn_s0, grid):
    sh1, sc1, g1, sh2, sc2, g2 = jnp.split(mod, 6, axis=-1)
    h = rms_norm(x, lp['norm1_g']) * (1.0 + sc1) + sh1
    mix, rg_s, dn_s = mixer_block(h, lp, rg_h0, dn_s0, grid)
    x = x + g1 * mix
    h = rms_norm(x, lp['norm2_g']) * (1.0 + sc2) + sh2
    x = x + g2 * ffn(h)
    return x, rg_s, dn_s


def setup_inputs(seed: int = 0) -> dict:
    key = jax.random.key(seed)
    ks = iter(jax.random.split(key, 64))

    def nrm(shape, scale=1.0):
        return scale * jax.random.normal(next(ks), shape, F32)

    def gain(shape):
        return 1.0 + 0.05 * nrm(shape)

    u = jax.random.uniform(next(ks), (DEPTH, 2, RG_W), F32, 0.9, 0.999)
    s = u ** (1.0 / RG_C)
    rg_lam = jnp.log(s) - jnp.log1p(-s)
    dn_a_log = jnp.log(jax.random.uniform(next(ks), (DEPTH, 2, DN_HEADS), F32, 1.0, 16.0))
    dt = jnp.exp(jax.random.uniform(next(ks), (DEPTH, 2, DN_HEADS), F32, math.log(1e-3), math.log(0.1)))
    dn_dt_bias = dt + jnp.log(-jnp.expm1(-dt))
    return {
        'x_prompt': nrm((BATCH, SEQ, D_MODEL)),
        'x_sample': nrm((DEC_BATCH, DEC_SEQ, D_MODEL)),
        'state_rglru': nrm((DEC_BATCH, DEPTH, 2, RG_W)),
        'state_delta': nrm((DEC_BATCH, DEPTH, 2, DN_HEADS, DN_DK, DN_DV), DN_DK ** -0.5),
        'c': nrm((DEC_BATCH, D_MODEL)),
        'c_ctx': nrm((D_MODEL,)),
        'ada_w': nrm((DEPTH, D_MODEL, 6 * D_MODEL), 0.5 * D_MODEL ** -0.5),
        'ada_b': nrm((DEPTH, 6 * D_MODEL), 0.01),
        'norm1_g': gain((DEPTH, D_MODEL)),
        'norm2_g': gain((DEPTH, D_MODEL)),
        'w_in': nrm((DEPTH, D_MODEL, N_IN), D_MODEL ** -0.5),
        'rg_conv_w': nrm((DEPTH, 2, CONV_W, RG_W), CONV_W ** -0.5),
        'rg_conv_b': nrm((DEPTH, 2, RG_W), 0.01),
        'rg_wa': nrm((DEPTH, 2, RG_HEADS, RG_HD, RG_HD), RG_HD ** -0.5),
        'rg_ba': nrm((DEPTH, 2, RG_W), 0.01),
        'rg_wx': nrm((DEPTH, 2, RG_HEADS, RG_HD, RG_HD), RG_HD ** -0.5),
        'rg_bx': nrm((DEPTH, 2, RG_W), 0.01),
        'rg_lam': rg_lam,
        'pool_w': nrm((DEPTH, len(POOL_WINDOWS), POOL_G, POOL_G), POOL_G ** -0.5),
        'pool_scale': gain((DEPTH, BR_W)),
        'sgu_ln_g': gain((DEPTH, SGU_W)),
        'sgu_ln_b': nrm((DEPTH, SGU_W), 0.01),
        'sgu_ws': nrm((DEPTH, SGU_HEADS, SGU_CHUNK, SGU_CHUNK), SGU_CHUNK ** -0.5),
        'sgu_bs': 1.0 + nrm((DEPTH, SGU_HEADS, SGU_CHUNK), 0.01),
        'dn_conv_w': nrm((DEPTH, 2, CONV_W, DN_QKV), CONV_W ** -0.5),
        'dn_a_log': dn_a_log,
        'dn_dt_bias': dn_dt_bias,
        'dn_norm_g': gain((DEPTH, DN_DV)),
        'w_br': nrm((DEPTH, N_BRANCH, BR_W, D_MODEL), BR_W ** -0.5),
        'w_out': nrm((DEPTH, D_MODEL, D_MODEL), D_MODEL ** -0.5),
        'ffn_w1': nrm((N_DENSE, D_MODEL, D_FF), D_MODEL ** -0.5),
        'ffn_w3': nrm((N_DENSE, D_MODEL, D_FF), D_MODEL ** -0.5),
        'ffn_w2': nrm((N_DENSE, D_FF, D_MODEL), D_FF ** -0.5),
        'moe_wr': nrm((N_MOE, D_MODEL, N_EXPERTS), D_MODEL ** -0.5),
        'moe_br': nrm((N_MOE, N_EXPERTS), 0.01),
        'moe_w1': nrm((N_MOE, N_EXPERTS, D_MODEL, D_FF_EXPERT), D_MODEL ** -0.5),
        'moe_w3': nrm((N_MOE, N_EXPERTS, D_MODEL, D_FF_EXPERT), D_MODEL ** -0.5),
        'moe_w2': nrm((N_MOE, N_EXPERTS, D_FF_EXPERT, D_MODEL), D_FF_EXPERT ** -0.5),
        'final_g': gain((D_MODEL,)),
    }


def reference(x_prompt, x_sample, state_rglru, state_delta, c, c_ctx, ada_w, ada_b, norm1_g, norm2_g,
              w_in, rg_conv_w, rg_conv_b, rg_wa, rg_ba, rg_wx, rg_bx, rg_lam, pool_w, pool_scale,
              sgu_ln_g, sgu_ln_b, sgu_ws, sgu_bs, dn_conv_w, dn_a_log, dn_dt_bias, dn_norm_g,
              w_br, w_out, ffn_w1, ffn_w3, ffn_w2, moe_wr, moe_br, moe_w1, moe_w3, moe_w2, final_g):
    bp = x_prompt.shape[0]
    rg_zero = jnp.zeros((bp, 2, RG_W), F32)
    dn_zero = jnp.zeros((bp, 2, DN_HEADS, DN_DK, DN_DV), F32)
    xp, xs = x_prompt, x_sample
    rg_new, dn_new = [], []
    for l in range(DEPTH):
        lp = {
            'norm1_g': norm1_g[l], 'norm2_g': norm2_g[l], 'w_in': w_in[l],
            'rg_conv_w': rg_conv_w[l], 'rg_conv_b': rg_conv_b[l], 'rg_wa': rg_wa[l], 'rg_ba': rg_ba[l],
            'rg_wx': rg_wx[l], 'rg_bx': rg_bx[l], 'rg_lam': rg_lam[l],
            'pool_w': pool_w[l], 'pool_scale': pool_scale[l],
            'sgu_ln_g': sgu_ln_g[l], 'sgu_ln_b': sgu_ln_b[l], 'sgu_ws': sgu_ws[l], 'sgu_bs': sgu_bs[l],
            'dn_conv_w': dn_conv_w[l], 'dn_a_log': dn_a_log[l], 'dn_dt_bias': dn_dt_bias[l],
            'dn_norm_g': dn_norm_g[l], 'w_br': w_br[l], 'w_out': w_out[l],
        }
        j = l // 2
        if l % 2 == 0:
            ffn = functools.partial(swiglu, w1=ffn_w1[j], w3=ffn_w3[j], w2=ffn_w2[j])
        else:
            ffn = functools.partial(moe_swiglu, wr=moe_wr[j], br=moe_br[j], w1=moe_w1[j], w3=moe_w3[j], w2=moe_w2[j])
        mod_ctx = (jax.nn.silu(c_ctx) @ ada_w[l] + ada_b[l])[None, None, :]
        mod_lat = (jax.nn.silu(c) @ ada_w[l] + ada_b[l])[:, None, :]
        xp, rg_s, dn_s = trunk_layer(xp, mod_ctx, lp, ffn, rg_zero, dn_zero, False)
        xs, _, _ = trunk_layer(xs, mod_lat, lp, ffn, state_rglru[:, l], state_delta[:, l], True)
        rg_new.append(rg_s.astype(x_prompt.dtype))
        dn_new.append(dn_s.astype(x_prompt.dtype))
    y_prompt = rms_norm(xp, final_g)
    y_sample = rms_norm(xs, final_g)
    new_state_rglru = jnp.stack(rg_new, axis=1)
    new_state_delta = jnp.stack(dn_new, axis=1)
    return (y_prompt, y_sample, new_state_rglru, new_state_delta)
```

```python
import functools

import jax
import jax.numpy as jnp
from jax import lax
from jax.experimental import pallas as pl
from jax.experimental.pallas import tpu as pltpu

F32 = jnp.float32
BF16 = jnp.bfloat16
HIGHEST = lax.Precision.HIGHEST

D_MODEL = 2048
DEPTH = 2
N_CTX_SEQ = 32
CTX_LEN = 256
N_LAT_SEQ = 2
LAT_LEN = 4096
GRID_W = 64
N_CTX_ROWS = N_CTX_SEQ * CTX_LEN
N_ROWS = N_CTX_ROWS + N_LAT_SEQ * LAT_LEN
BR_W = 512
RG_C = 8.0
POOL_WINDOWS = (2, 4, 8, 16)
SGU_CHUNK = 128
DN_CHUNK = 64
DN_HEADS = 4
HEAD_W = 128
N_EXPERTS = 8
EPS = 1e-6

SEQ_TILE = 256
N_SEQ_TILES = N_ROWS // SEQ_TILE
CTX_TILES = N_CTX_ROWS // SEQ_TILE
LAT_TILES_PER_SEQ = LAT_LEN // SEQ_TILE
HALO = 8

COL_BLK = 512
CB_RG_X, CB_RG_G, CB_POOL, CB_SGU_U, CB_SGU_V, CB_DN_Q, CB_DN_K, CB_DN_V, CB_DN_Z = range(9)
COL_BA = 4608
CB_GATE0 = 10
P_COLS = 5120 + 4 * D_MODEL

VMEM_LIMIT = 56 * 1024 * 1024


def _cparams(sem):
    return pltpu.CompilerParams(dimension_semantics=sem, vmem_limit_bytes=VMEM_LIMIT)


def _row_group(row0):
    return jnp.where(row0 < N_CTX_ROWS, 0, 1 + (row0 - N_CTX_ROWS) // LAT_LEN)


def _tile_seq(tt):
    return jnp.where(tt < CTX_TILES, tt, CTX_TILES + (tt - CTX_TILES) // LAT_TILES_PER_SEQ)


def _expm1(x):
    poly = x * (1.0 + x * (0.5 + x * (1.0 / 6.0 + x * (1.0 / 24.0 + x * (1.0 / 120.0 + x * (1.0 / 720.0))))))
    return jnp.where(jnp.abs(x) < 0.1, poly, jnp.exp(x) - 1.0)


def _rms(x, g):
    return x * lax.rsqrt(jnp.mean(x * x, axis=-1, keepdims=True) + EPS) * g


def _ada_body(c_ref, w_ref, b_ref, o_ref):
    c = c_ref[...]
    s = (c * jax.nn.sigmoid(c)).astype(BF16)
    o_ref[0] = jnp.dot(s, w_ref[0].astype(BF16), preferred_element_type=F32) + b_ref[0]


def _ada(c_all, ada_w, ada_b):
    tn = 512
    nj = 6 * D_MODEL // tn
    return pl.pallas_call(
        _ada_body,
        grid=(DEPTH, nj),
        in_specs=[
            pl.BlockSpec((8, D_MODEL), lambda l, j: (0, 0)),
            pl.BlockSpec((1, D_MODEL, tn), lambda l, j: (l, 0, j)),
            pl.BlockSpec((1, 1, tn), lambda l, j: (l, 0, j)),
        ],
        out_specs=pl.BlockSpec((1, 8, tn), lambda l, j: (l, 0, j)),
        out_shape=jax.ShapeDtypeStruct((DEPTH, 8, 6 * D_MODEL), F32),
        compiler_params=_cparams(("arbitrary", "arbitrary")),
        name="ada_mod",
    )(c_all, ada_w, ada_b.reshape(DEPTH, 1, 6 * D_MODEL))


IN_TM = 1024
IN_TN = 1024


def _inproj_body(x_ref, mod_ref, g_ref, w_ref, o_ref, h_ref):
    @pl.when(pl.program_id(1) == 0)
    def _():
        y = _rms(x_ref[...], g_ref[...])
        h = y * (1.0 + mod_ref[0, 1:2, :]) + mod_ref[0, 0:1, :]
        h_ref[...] = h.astype(BF16)

    o_ref[...] = jnp.dot(h_ref[...], w_ref[...], preferred_element_type=F32)


def _inproj(x, mod, norm_g, w_in_p):
    return pl.pallas_call(
        _inproj_body,
        grid=(N_ROWS // IN_TM, P_COLS // IN_TN),
        in_specs=[
            pl.BlockSpec((IN_TM, D_MODEL), lambda i, j: (i, 0)),
            pl.BlockSpec((1, 6, D_MODEL), lambda i, j: (_row_group(i * IN_TM), 0, 0)),
            pl.BlockSpec((1, D_MODEL), lambda i, j: (0, 0)),
            pl.BlockSpec((D_MODEL, IN_TN), lambda i, j: (0, j)),
        ],
        out_specs=pl.BlockSpec((IN_TM, IN_TN), lambda i, j: (i, j)),
        out_shape=jax.ShapeDtypeStruct((N_ROWS, P_COLS), F32),
        scratch_shapes=[pltpu.VMEM((IN_TM, D_MODEL), BF16)],
        compiler_params=_cparams(("arbitrary", "arbitrary")),
        name="in_proj",
    )(x, mod, norm_g.reshape(1, D_MODEL), w_in_p)


def _scan_tile(rev, t):
    tt = (N_SEQ_TILES - 1 - t) if rev else t
    edge = (LAT_TILES_PER_SEQ - 1) if rev else 0
    first = jnp.logical_or(tt < CTX_TILES, ((tt - CTX_TILES) % LAT_TILES_PER_SEQ) == edge)
    return tt, first


def _conv_taps(rev, x, halo):
    n = x.shape[0]
    if not rev:
        ext = jnp.concatenate([halo, x], axis=0)
        return lambda k: x if k == 0 else pltpu.roll(ext, k, axis=0)[HALO:]
    ext = jnp.concatenate([x, halo], axis=0)
    return lambda k: x if k == 0 else pltpu.roll(ext, n + HALO - k, axis=0)[:n]


def _short_conv(rev, x, halo, cw):
    tap = _conv_taps(rev, x, halo)
    acc = tap(3) * cw[0:1, :]
    for j in range(1, 4):
        acc = acc + tap(3 - j) * cw[j:j + 1, :]
    return acc


def _next_halo(rev, x):
    return x[0:HALO] if rev else x[x.shape[0] - HALO:]


def _rglru_body(rev, *refs):
    if rev:
        (xr_ref, hf_ref, xg_ref, h0_ref, cw_ref, cb_ref, wa_ref, ba_ref, wx_ref, bx_ref, lam_ref,
         out_ref, st_ref, halo_ref, carry_ref) = refs
    else:
        (xr_ref, h0_ref, cw_ref, cb_ref, wa_ref, ba_ref, wx_ref, bx_ref, lam_ref,
         out_ref, st_ref, halo_ref, carry_ref) = refs
    _, first = _scan_tile(rev, pl.program_id(0))

    @pl.when(first)
    def _():
        halo_ref[...] = jnp.zeros_like(halo_ref)
        carry_ref[...] = h0_ref[0]

    x = xr_ref[...]
    xc = _short_conv(rev, x, halo_ref[...], cw_ref[...]) + cb_ref[...]
    halo_ref[...] = _next_halo(rev, x)

    xcb = xc.astype(BF16)
    rs, gs = [], []
    for hh in range(4):
        xh = xcb[:, HEAD_W * hh:HEAD_W * (hh + 1)]
        rs.append(jnp.dot(xh, wa_ref[hh], preferred_element_type=F32))
        gs.append(jnp.dot(xh, wx_ref[hh], preferred_element_type=F32))
    r = jax.nn.sigmoid(jnp.concatenate(rs, axis=1) + ba_ref[...])
    gi = jax.nn.sigmoid(jnp.concatenate(gs, axis=1) + bx_ref[...])
    log_a = (-RG_C * jax.nn.softplus(-lam_ref[...])) * r
    a = jnp.exp(log_a)
    b = jnp.sqrt(-_expm1(2.0 * log_a)) * gi * xc

    n = SEQ_TILE
    row = lax.broadcasted_iota(jnp.int32, (n, BR_W), 0)
    k = 1
    while k < n:
        if rev:
            keep = row < n - k
            a_s = jnp.where(keep, pltpu.roll(a, n - k, axis=0), 1.0)
            b_s = jnp.where(keep, pltpu.roll(b, n - k, axis=0), 0.0)
        else:
            keep = row >= k
            a_s = jnp.where(keep, pltpu.roll(a, k, axis=0), 1.0)
            b_s = jnp.where(keep, pltpu.roll(b, k, axis=0), 0.0)
        b = a * b_s + b
        a = a * a_s
        k *= 2
    h = b + a * carry_ref[...]
    last = h[0:1] if rev else h[n - 1:n]
    carry_ref[...] = last
    st_ref[0] = last
    if rev:
        out_ref[...] = (jax.nn.gelu(xg_ref[...]) * (hf_ref[...] + h)).astype(BF16)
    else:
        out_ref[...] = h


def _rglru_dir(rev, proj, hf, h0, cw, cb, wa, ba, wx, bx, lam):
    def tile_map(t):
        return (N_SEQ_TILES - 1 - t) if rev else t

    row_blk = lambda cb_: pl.BlockSpec((SEQ_TILE, COL_BLK), lambda t: (tile_map(t), cb_))
    full = lambda shp: pl.BlockSpec(shp, lambda t: (0,) * len(shp))
    in_specs = [row_blk(CB_RG_X)]
    args = [proj]
    if rev:
        in_specs += [pl.BlockSpec((SEQ_TILE, BR_W), lambda t: (tile_map(t), 0)), row_blk(CB_RG_G)]
        args += [hf, proj]
    in_specs += [
        pl.BlockSpec((1, 1, BR_W), lambda t: (_tile_seq(tile_map(t)), 0, 0)),
        full((4, BR_W)), full((1, BR_W)), full((4, HEAD_W, HEAD_W)), full((1, BR_W)),
        full((4, HEAD_W, HEAD_W)), full((1, BR_W)), full((1, BR_W)),
    ]
    args += [h0, cw, cb.reshape(1, BR_W), wa.astype(BF16), ba.reshape(1, BR_W), wx.astype(BF16),
             bx.reshape(1, BR_W), lam.reshape(1, BR_W)]
    return pl.pallas_call(
        functools.partial(_rglru_body, rev),
        grid=(N_SEQ_TILES,),
        in_specs=in_specs,
        out_specs=[
            pl.BlockSpec((SEQ_TILE, BR_W), lambda t: (tile_map(t), 0)),
            pl.BlockSpec((1, 1, BR_W), lambda t: (tile_map(t), 0, 0)),
        ],
        out_shape=[
            jax.ShapeDtypeStruct((N_ROWS, BR_W), BF16 if rev else F32),
            jax.ShapeDtypeStruct((N_SEQ_TILES, 1, BR_W), F32),
        ],
        scratch_shapes=[pltpu.VMEM((HALO, BR_W), F32), pltpu.VMEM((1, BR_W), F32)],
        compiler_params=_cparams(("arbitrary",)),
        name="rglru_bwd" if rev else "rglru_fwd",
    )(*args)


def _pool_body(x_ref, w_ref, sc_ref, o_ref):
    shift = jnp.where(pl.program_id(0) < CTX_TILES, 8, 6)
    t = lax.broadcasted_iota(jnp.int32, (SEQ_TILE, SEQ_TILE), 0)
    s = lax.broadcasted_iota(jnp.int32, (SEQ_TILE, SEQ_TILE), 1)
    same = jnp.right_shift(t, shift) == jnp.right_shift(s, shift)
    d = s - t
    x = x_ref[...]
    outs = []
    for j, w in enumerate(POOL_WINDOWS):
        inwin = jnp.logical_and(jnp.logical_and(d >= -(w // 2), d <= w - 1 - w // 2), same)
        m = jnp.where(inwin, 1.0, 0.0)
        cnt = jnp.sum(m, axis=1, keepdims=True)
        xg = x[:, HEAD_W * j:HEAD_W * (j + 1)]
        ssum = jnp.dot(m, xg, preferred_element_type=F32, precision=HIGHEST)
        pooled = ssum / cnt - xg
        outs.append(jnp.dot(pooled.astype(BF16), w_ref[j], preferred_element_type=F32))
    o_ref[...] = (jnp.concatenate(outs, axis=1) * sc_ref[...]).astype(BF16)


def _pool(proj, w_pool, scale):
    return pl.pallas_call(
        _pool_body,
        grid=(N_SEQ_TILES,),
        in_specs=[
            pl.BlockSpec((SEQ_TILE, COL_BLK), lambda t: (t, CB_POOL)),
            pl.BlockSpec((4, HEAD_W, HEAD_W), lambda t: (0, 0, 0)),
            pl.BlockSpec((1, BR_W), lambda t: (0, 0)),
        ],
        out_specs=pl.BlockSpec((SEQ_TILE, BR_W), lambda t: (t, 0)),
        out_shape=jax.ShapeDtypeStruct((N_ROWS, BR_W), BF16),
        compiler_params=_cparams(("arbitrary",)),
        name="pool_mixer",
    )(proj, w_pool.astype(BF16), scale.reshape(1, BR_W))


def _sgu_body(u_ref, v_ref, lg_ref, lb_ref, ws_ref, bst_ref, o_ref):
    u = jax.nn.gelu(u_ref[...])
    v = jax.nn.gelu(v_ref[...])
    mu = jnp.mean(v, axis=-1, keepdims=True)
    var = jnp.mean(jnp.square(v - mu), axis=-1, keepdims=True)
    vb = ((v - mu) * lax.rsqrt(var + EPS) * lg_ref[...] + lb_ref[...]).astype(BF16)
    rows = []
    for cc in range(SEQ_TILE // SGU_CHUNK):
        cols = []
        for hh in range(4):
            blk = vb[SGU_CHUNK * cc:SGU_CHUNK * (cc + 1), HEAD_W * hh:HEAD_W * (hh + 1)]
            cols.append(jnp.dot(ws_ref[hh], blk, preferred_element_type=F32) + bst_ref[:, hh:hh + 1])
        rows.append(jnp.concatenate(cols, axis=1))
    o_ref[...] = (u * jnp.concatenate(rows, axis=0)).astype(BF16)


def _sgu(proj, ln_g, ln_b, ws, bs):
    return pl.pallas_call(
        _sgu_body,
        grid=(N_SEQ_TILES,),
        in_specs=[
            pl.BlockSpec((SEQ_TILE, COL_BLK), lambda t: (t, CB_SGU_U)),
            pl.BlockSpec((SEQ_TILE, COL_BLK), lambda t: (t, CB_SGU_V)),
            pl.BlockSpec((1, BR_W), lambda t: (0, 0)),
            pl.BlockSpec((1, BR_W), lambda t: (0, 0)),
            pl.BlockSpec((4, SGU_CHUNK, SGU_CHUNK), lambda t: (0, 0, 0)),
            pl.BlockSpec((SGU_CHUNK, 4), lambda t: (0, 0)),
        ],
        out_specs=pl.BlockSpec((SEQ_TILE, BR_W), lambda t: (t, 0)),
        out_shape=jax.ShapeDtypeStruct((N_ROWS, BR_W), BF16),
        compiler_params=_cparams(("arbitrary",)),
        name="sgu_mixer",
    )(proj, proj, ln_g.reshape(1, BR_W), ln_b.reshape(1, BR_W), ws.astype(BF16), bs.T)


def _dot_nt(a, b):
    return lax.dot_general(a, b, (((1,), (1,)), ((), ())), preferred_element_type=F32)


def _dot_tn(a, b):
    return lax.dot_general(a, b, (((0,), (0,)), ((), ())), preferred_element_type=F32)


def _dot_hi(a, b):
    return jnp.dot(a, b, preferred_element_type=F32, precision=HIGHEST)


def _l2norm(x):
    return x * lax.rsqrt(jnp.sum(x * x, axis=-1, keepdims=True) + EPS)


def _dn_body(rev, *refs):
    if rev:
        (q_ref, k_ref, v_ref, ba_ref, of_ref, z_ref, s0_ref, cw_ref, alog_ref, dt_ref, ng_ref,
         out_ref, sfin_ref, halo_ref, s_ref) = refs
    else:
        (q_ref, k_ref, v_ref, ba_ref, s0_ref, cw_ref, alog_ref, dt_ref,
         out_ref, sfin_ref, halo_ref, s_ref) = refs
    _, first = _scan_tile(rev, pl.program_id(0))

    @pl.when(first)
    def _():
        halo_ref[...] = jnp.zeros_like(halo_ref)
        s_ref[...] = s0_ref[0]

    ys = []
    for p, ref in enumerate((q_ref, k_ref, v_ref)):
        x = ref[...]
        y = _short_conv(rev, x, halo_ref[p], cw_ref[:, BR_W * p:BR_W * (p + 1)])
        halo_ref[p] = _next_halo(rev, x)
        ys.append(y * jax.nn.sigmoid(y))
    yq, yk, yv = ys

    ba = ba_ref[...]
    beta_all = jax.nn.sigmoid(ba)
    g_all = -jnp.exp(alog_ref[...]) * jax.nn.softplus(ba + dt_ref[...])
    d_off = 8 if rev else 0

    C = DN_CHUNK
    ci = lax.broadcasted_iota(jnp.int32, (C, C), 0)
    cj = lax.broadcasted_iota(jnp.int32, (C, C), 1)
    incl = (ci <= cj) if rev else (ci >= cj)
    strict = (ci < cj) if rev else (ci > cj)
    tri = jnp.where(incl, 1.0, 0.0)
    eye = jnp.where(ci == cj, 1.0, 0.0)
    n_chunks = SEQ_TILE // C
    order = range(n_chunks - 1, -1, -1) if rev else range(n_chunks)

    pre = {}
    for c in range(n_chunks):
        rs = slice(C * c, C * (c + 1))
        gcum = _dot_hi(tri, g_all[rs])
        gcum_t = gcum.T
        for hh in range(DN_HEADS):
            ls = slice(HEAD_W * hh, HEAD_W * (hh + 1))
            col = d_off + 4 + hh
            g_col = gcum[:, col:col + 1]
            g_row = gcum_t[col:col + 1, :]
            g_last = gcum[0:1, col:col + 1] if rev else gcum[C - 1:C, col:col + 1]
            beta = beta_all[rs, d_off + hh:d_off + hh + 1]
            qh = _l2norm(yq[rs, ls]) * (HEAD_W ** -0.5)
            kh = _l2norm(yk[rs, ls])
            vh = yv[rs, ls]
            decay = jnp.where(incl, jnp.exp(jnp.minimum(g_col - g_row, 0.0)), 0.0)
            eg = jnp.exp(g_col)
            kb = kh * beta
            kh16 = kh.astype(BF16)
            nmat = jnp.where(strict, _dot_nt(kb.astype(BF16), kh16) * decay, 0.0)
            xinv = eye - nmat
            pw = _dot_hi(nmat, nmat)
            for it in range(5):
                xinv = xinv + _dot_hi(xinv, pw)
                if it < 4:
                    pw = _dot_hi(pw, pw)
            rhs = jnp.concatenate([vh * beta, kb * eg], axis=1)
            w = _dot_hi(xinv, rhs)
            pre[(c, hh)] = dict(
                val=w[:, :HEAD_W],
                kcum=w[:, HEAD_W:].astype(BF16),
                qk=(_dot_nt(qh.astype(BF16), kh16) * decay).astype(BF16),
                qg=(qh * eg).astype(BF16),
                kend=(kh * jnp.exp(g_last - g_col)).astype(BF16),
                gend=jnp.exp(g_last),
            )

    o_blocks = {}
    for hh in range(DN_HEADS):
        s = s_ref[hh]
        for c in order:
            p = pre[(c, hh)]
            s16 = s.astype(BF16)
            u = p["val"] - jnp.dot(p["kcum"], s16, preferred_element_type=F32)
            u16 = u.astype(BF16)
            o_blocks[(c, hh)] = (jnp.dot(p["qg"], s16, preferred_element_type=F32)
                                 + jnp.dot(p["qk"], u16, preferred_element_type=F32))
            s = s * p["gend"] + _dot_tn(p["kend"], u16)
        s_ref[hh] = s
        sfin_ref[0, hh] = s

    o = jnp.concatenate(
        [jnp.concatenate([o_blocks[(c, hh)] for hh in range(DN_HEADS)], axis=1) for c in range(n_chunks)], axis=0)
    if rev:
        o = o + of_ref[...]
        z = z_ref[...]
        cols = []
        for hh in range(DN_HEADS):
            oh = o[:, HEAD_W * hh:HEAD_W * (hh + 1)]
            cols.append(oh * lax.rsqrt(jnp.mean(oh * oh, axis=-1, keepdims=True) + EPS) * ng_ref[...])
        out_ref[...] = (jnp.concatenate(cols, axis=1) * (z * jax.nn.sigmoid(z))).astype(BF16)
    else:
        out_ref[...] = o


def _dn_dir(rev, proj, of, s0, cw, alog_row, dt_row, norm_g):
    def tile_map(t):
        return (N_SEQ_TILES - 1 - t) if rev else t

    row_blk = lambda cb_: pl.BlockSpec((SEQ_TILE, COL_BLK), lambda t: (tile_map(t), cb_))
    full = lambda shp: pl.BlockSpec(shp, lambda t: (0,) * len(shp))
    in_specs = [row_blk(CB_DN_Q), row_blk(CB_DN_K), row_blk(CB_DN_V),
                pl.BlockSpec((SEQ_TILE, HEAD_W), lambda t: (tile_map(t), COL_BA // HEAD_W))]
    args = [proj, proj, proj, proj]
    if rev:
        in_specs += [pl.BlockSpec((SEQ_TILE, BR_W), lambda t: (tile_map(t), 0)), row_blk(CB_DN_Z)]
        args += [of, proj]
    in_specs += [
        pl.BlockSpec((1, DN_HEADS, HEAD_W, HEAD_W), lambda t: (_tile_seq(tile_map(t)), 0, 0, 0)),
        full((4, 3 * BR_W)), full((1, HEAD_W)), full((1, HEAD_W)),
    ]
    args += [s0, cw, alog_row, dt_row]
    if rev:
        in_specs += [full((1, HEAD_W))]
        args += [norm_g.reshape(1, HEAD_W)]
    return pl.pallas_call(
        functools.partial(_dn_body, rev),
        grid=(N_SEQ_TILES,),
        in_specs=in_specs,
        out_specs=[
            pl.BlockSpec((SEQ_TILE, BR_W), lambda t: (tile_map(t), 0)),
            pl.BlockSpec((1, DN_HEADS, HEAD_W, HEAD_W), lambda t: (jnp.minimum(tile_map(t), CTX_TILES), 0, 0, 0)),
        ],
        out_shape=[
            jax.ShapeDtypeStruct((N_ROWS, BR_W), BF16 if rev else F32),
            jax.ShapeDtypeStruct((CTX_TILES + 1, DN_HEADS, HEAD_W, HEAD_W), F32),
        ],
        scratch_shapes=[pltpu.VMEM((3, HALO, BR_W), F32), pltpu.VMEM((DN_HEADS, HEAD_W, HEAD_W), F32)],
        compiler_params=_cparams(("arbitrary",)),
        name="deltanet_bwd" if rev else "deltanet_fwd",
    )(*args)


MG_TM = 512
MG_TN = 512


def _merge_body(x_ref, mod_ref, ya_ref, yb_ref, yc_ref, yd_ref, g0_ref, g1_ref, g2_ref, g3_ref,
                wbr_ref, wout_ref, o_ref, acc_ref):
    j = pl.program_id(1)

    @pl.when(j == 0)
    def _():
        acc_ref[...] = jnp.zeros_like(acc_ref)

    m = None
    for k, (y_ref, g_ref) in enumerate(((ya_ref, g0_ref), (yb_ref, g1_ref), (yc_ref, g2_ref), (yd_ref, g3_ref))):
        br = jnp.dot(y_ref[...], wbr_ref[k], preferred_element_type=F32)
        term = jax.nn.sigmoid(g_ref[...]) * br
        m = term if m is None else m + term
    acc_ref[...] += jnp.dot(m.astype(BF16), wout_ref[...], preferred_element_type=F32)

    @pl.when(j == pl.num_programs(1) - 1)
    def _():
        o_ref[...] = x_ref[...] + mod_ref[0, 2:3, :] * acc_ref[...]


def _merge(x, mod, ys, proj, w_br, w_out):
    nj = D_MODEL // MG_TN
    y_spec = pl.BlockSpec((MG_TM, BR_W), lambda i, j: (i, 0))
    gate_spec = lambda k: pl.BlockSpec((MG_TM, MG_TN), lambda i, j: (i, CB_GATE0 + k * nj + j))
    return pl.pallas_call(
        _merge_body,
        grid=(N_ROWS // MG_TM, nj),
        in_specs=[
            pl.BlockSpec((MG_TM, D_MODEL), lambda i, j: (i, 0)),
            pl.BlockSpec((1, 6, D_MODEL), lambda i, j: (_row_group(i * MG_TM), 0, 0)),
            y_spec, y_spec, y_spec, y_spec,
            gate_spec(0), gate_spec(1), gate_spec(2), gate_spec(3),
            pl.BlockSpec((4, BR_W, MG_TN), lambda i, j: (0, 0, j)),
            pl.BlockSpec((MG_TN, D_MODEL), lambda i, j: (j, 0)),
        ],
        out_specs=pl.BlockSpec((MG_TM, D_MODEL), lambda i, j: (i, 0)),
        out_shape=jax.ShapeDtypeStruct((N_ROWS, D_MODEL), F32),
        scratch_shapes=[pltpu.VMEM((MG_TM, D_MODEL), F32)],
        compiler_params=_cparams(("arbitrary", "arbitrary")),
        name="merge_out",
    )(x, mod, *ys, proj, proj, proj, proj, w_br, w_out)


N2_TM = 512


def _norm2_body(moe, *refs):
    if moe:
        x_ref, mod_ref, g_ref, wr_ref, br_ref, h_ref, gate_ref = refs
    else:
        x_ref, mod_ref, g_ref, h_ref = refs
    y = _rms(x_ref[...], g_ref[...])
    h = y * (1.0 + mod_ref[0, 4:5, :]) + mod_ref[0, 3:4, :]
    h_ref[...] = h.astype(BF16)
    if moe:
        logits = jnp.dot(h, wr_ref[...], preferred_element_type=F32, precision=HIGHEST) + br_ref[...]
        idx = lax.broadcasted_iota(jnp.int32, logits.shape, 1)
        m1 = jnp.max(logits, axis=-1, keepdims=True)
        i1 = jnp.min(jnp.where(logits == m1, idx, N_EXPERTS), axis=-1, keepdims=True)
        rest = jnp.where(idx == i1, -jnp.inf, logits)
        m2 = jnp.max(rest, axis=-1, keepdims=True)
        i2 = jnp.min(jnp.where(rest == m2, idx, N_EXPERTS), axis=-1, keepdims=True)
        e2 = jnp.exp(m2 - m1)
        w1 = 1.0 / (1.0 + e2)
        gate_ref[...] = jnp.where(idx == i1, w1, 0.0) + jnp.where(idx == i2, e2 * w1, 0.0)


def _norm2(x, mod, norm_g, wr=None, br=None):
    moe = wr is not None
    in_specs = [
        pl.BlockSpec((N2_TM, D_MODEL), lambda i: (i, 0)),
        pl.BlockSpec((1, 6, D_MODEL), lambda i: (_row_group(i * N2_TM), 0, 0)),
        pl.BlockSpec((1, D_MODEL), lambda i: (0, 0)),
    ]
    args = [x, mod, norm_g.reshape(1, D_MODEL)]
    out_specs = [pl.BlockSpec((N2_TM, D_MODEL), lambda i: (i, 0))]
    out_shape = [jax.ShapeDtypeStruct((N_ROWS, D_MODEL), BF16)]
    if moe:
        in_specs += [pl.BlockSpec((D_MODEL, N_EXPERTS), lambda i: (0, 0)), pl.BlockSpec((1, N_EXPERTS), lambda i: (0, 0))]
        args += [wr, br.reshape(1, N_EXPERTS)]
        out_specs += [pl.BlockSpec((N2_TM, N_EXPERTS), lambda i: (i, 0))]
        out_shape += [jax.ShapeDtypeStruct((N_ROWS, N_EXPERTS), F32)]
    return pl.pallas_call(
        functools.partial(_norm2_body, moe),
        grid=(N_ROWS // N2_TM,),
        in_specs=in_specs,
        out_specs=out_specs,
        out_shape=out_shape,
        compiler_params=_cparams(("arbitrary",)),
        name="norm2_router" if moe else "norm2",
    )(*args)


FF_TM = 512


def _ffn_body(n_exp, nf, final, *refs):
    if n_exp > 1:
        h_ref, gate_ref, x_ref, mod_ref, fg_ref, w1_ref, w3_ref, w2_ref, o_ref, acc_ref = refs
    else:
        h_ref, x_ref, mod_ref, fg_ref, w1_ref, w3_ref, w2_ref, o_ref, acc_ref = refs
    s = pl.program_id(1)

    @pl.when(s == 0)
    def _():
        acc_ref[...] = jnp.zeros_like(acc_ref)

    h = h_ref[...]
    a1 = jnp.dot(h, w1_ref[0], preferred_element_type=F32)
    a3 = jnp.dot(h, w3_ref[0], preferred_element_type=F32)
    act = a1 * jax.nn.sigmoid(a1) * a3
    if n_exp > 1:
        gate = gate_ref[...]
        lane = lax.broadcasted_iota(jnp.int32, gate.shape, 1)
        act = act * jnp.sum(jnp.where(lane == s // nf, gate, 0.0), axis=-1, keepdims=True)
    acc_ref[...] += jnp.dot(act.astype(BF16), w2_ref[0], preferred_element_type=F32)

    @pl.when(s == pl.num_programs(1) - 1)
    def _():
        xn = x_ref[...] + mod_ref[0, 5:6, :] * acc_ref[...]
        o_ref[...] = _rms(xn, fg_ref[...]) if final else xn


def _ffn(h, gate, x, mod, final_g, w1, w3, w2, tf, final):
    n_exp, _, ff = w1.shape
    nf = ff // tf
    in_specs = [pl.BlockSpec((FF_TM, D_MODEL), lambda i, s: (i, 0))]
    args = [h]
    if n_exp > 1:
        in_specs += [pl.BlockSpec((FF_TM, N_EXPERTS), lambda i, s: (i, 0))]
        args += [gate]
    in_specs += [
        pl.BlockSpec((FF_TM, D_MODEL), lambda i, s: (i, 0)),
        pl.BlockSpec((1, 6, D_MODEL), lambda i, s: (_row_group(i * FF_TM), 0, 0)),
        pl.BlockSpec((1, D_MODEL), lambda i, s: (0, 0)),
        pl.BlockSpec((1, D_MODEL, tf), lambda i, s: (s // nf, 0, s % nf)),
        pl.BlockSpec((1, D_MODEL, tf), lambda i, s: (s // nf, 0, s % nf)),
        pl.BlockSpec((1, tf, D_MODEL), lambda i, s: (s // nf, s % nf, 0)),
    ]
    args += [x, mod, final_g.reshape(1, D_MODEL), w1, w3, w2]
    return pl.pallas_call(
        functools.partial(_ffn_body, n_exp, nf, final),
        grid=(N_ROWS // FF_TM, n_exp * nf),
        in_specs=in_specs,
        out_specs=pl.BlockSpec((FF_TM, D_MODEL), lambda i, s: (i, 0)),
        out_shape=jax.ShapeDtypeStruct((N_ROWS, D_MODEL), F32),
        scratch_shapes=[pltpu.VMEM((FF_TM, D_MODEL), F32)],
        compiler_params=_cparams(("arbitrary", "arbitrary")),
        name="moe_ffn" if n_exp > 1 else "dense_ffn",
    )(*args)


def _ba_row(v):
    row = jnp.zeros((1, HEAD_W), F32)
    for d in range(2):
        row = row.at[0, d * 8 + 4:d * 8 + 8].set(v[d])
    return row


def kernel(x_prompt, x_sample, state_rglru, state_delta, c, c_ctx, ada_w, ada_b, norm1_g, norm2_g, w_in, rg_conv_w, rg_conv_b, rg_wa, rg_ba, rg_wx, rg_bx, rg_lam, pool_w, pool_scale, sgu_ln_g, sgu_ln_b, sgu_ws, sgu_bs, dn_conv_w, dn_a_log, dn_dt_bias, dn_norm_g, w_br, w_out, ffn_w1, ffn_w3, ffn_w2, moe_wr, moe_br, moe_w1, moe_w3, moe_w2, final_g):
    x = jnp.concatenate([x_prompt.reshape(N_CTX_ROWS, D_MODEL), x_sample.reshape(N_ROWS - N_CTX_ROWS, D_MODEL)], axis=0)
    c_all = jnp.concatenate([c_ctx[None, :], c, jnp.zeros((8 - 1 - N_LAT_SEQ, D_MODEL), F32)], axis=0)
    mods = _ada(c_all, ada_w, ada_b).reshape(DEPTH, 8, 6, D_MODEL)

    rg_states, dn_states = [], []
    for l in range(DEPTH):
        mod = mods[l]
        w_in_p = jnp.concatenate(
            [w_in[l][:, :COL_BA + 16], jnp.zeros((D_MODEL, COL_BLK - 16), F32), w_in[l][:, COL_BA + 16:]],
            axis=1).astype(BF16)
        proj = _inproj(x, mod, norm1_g[l], w_in_p)

        rg_h0 = [jnp.concatenate([jnp.zeros((N_CTX_SEQ, BR_W), F32), state_rglru[:, l, d]], axis=0)[:, None, :]
                 for d in range(2)]
        rg_args = lambda d: (rg_conv_w[l, d], rg_conv_b[l, d], rg_wa[l, d], rg_ba[l, d], rg_wx[l, d],
                             rg_bx[l, d], rg_lam[l, d])
        hf, st_f = _rglru_dir(False, proj, None, rg_h0[0], *rg_args(0))
        y_a, st_b = _rglru_dir(True, proj, hf, rg_h0[1], *rg_args(1))
        rg_states.append(jnp.stack([st_f[:CTX_TILES, 0], st_b[:CTX_TILES, 0]], axis=1))

        y_b = _pool(proj, pool_w[l], pool_scale[l])
        y_c = _sgu(proj, sgu_ln_g[l], sgu_ln_b[l], sgu_ws[l], sgu_bs[l])

        dn_s0 = [jnp.concatenate([jnp.zeros((N_CTX_SEQ, DN_HEADS, HEAD_W, HEAD_W), F32), state_delta[:, l, d]], axis=0)
                 for d in range(2)]
        alog_row, dt_row = _ba_row(dn_a_log[l]), _ba_row(dn_dt_bias[l])
        of, s_f = _dn_dir(False, proj, None, dn_s0[0], dn_conv_w[l, 0], alog_row, dt_row, None)
        y_d, s_b = _dn_dir(True, proj, of, dn_s0[1], dn_conv_w[l, 1], alog_row, dt_row, dn_norm_g[l])
        dn_states.append(jnp.stack([s_f[:CTX_TILES], s_b[:CTX_TILES]], axis=1))

        x = _merge(x, mod, (y_a, y_b, y_c, y_d), proj, w_br[l].astype(BF16), w_out[l].astype(BF16))

        j = l // 2
        final = l == DEPTH - 1
        if l % 2 == 0:
            (h2,) = _norm2(x, mod, norm2_g[l])
            x = _ffn(h2, None, x, mod, final_g, ffn_w1[j][None].astype(BF16), ffn_w3[j][None].astype(BF16),
                     ffn_w2[j][None].astype(BF16), 512, final)
        else:
            h2, gate = _norm2(x, mod, norm2_g[l], moe_wr[j], moe_br[j])
            x = _ffn(h2, gate, x, mod, final_g, moe_w1[j].astype(BF16), moe_w3[j].astype(BF16),
                     moe_w2[j].astype(BF16), 256, final)

    y_prompt = x[:N_CTX_ROWS].reshape(N_CTX_SEQ, CTX_LEN, D_MODEL)
    y_sample = x[N_CTX_ROWS:].reshape(N_LAT_SEQ, LAT_LEN, D_MODEL)
    return (y_prompt, y_sample, jnp.stack(rg_states, axis=1), jnp.stack(dn_states, axis=1))
```

```python
import functools

import jax
import jax.numpy as jnp
from jax import lax
from jax.experimental import pallas as pl
from jax.experimental.pallas import tpu as pltpu

F32 = jnp.float32
BF16 = jnp.bfloat16
HIGHEST = lax.Precision.HIGHEST

D_MODEL = 2048
DEPTH = 2
N_CTX_SEQ = 32
CTX_LEN = 256
N_LAT_SEQ = 2
LAT_LEN = 4096
GRID_W = 64
N_CTX_ROWS = N_CTX_SEQ * CTX_LEN
N_ROWS = N_CTX_ROWS + N_LAT_SEQ * LAT_LEN
BR_W = 512
RG_C = 8.0
POOL_WINDOWS = (2, 4, 8, 16)
SGU_CHUNK = 128
DN_CHUNK = 64
DN_HEADS = 4
HEAD_W = 128
N_EXPERTS = 8
EPS = 1e-6

SEQ_TILE = 256
N_SEQ_TILES = N_ROWS // SEQ_TILE
CTX_TILES = N_CTX_ROWS // SEQ_TILE
LAT_TILES_PER_SEQ = LAT_LEN // SEQ_TILE
HALO = 8

COL_BLK = 512
CB_RG_X, CB_RG_G, CB_POOL, CB_SGU_U, CB_SGU_V, CB_DN_Q, CB_DN_K, CB_DN_V, CB_DN_Z = range(9)
COL_BA = 4608
CB_GATE0 = 10
P_COLS = 5120 + 4 * D_MODEL

VMEM_LIMIT = 56 * 1024 * 1024


def _cparams(sem):
    return pltpu.CompilerParams(dimension_semantics=sem, vmem_limit_bytes=VMEM_LIMIT)


def _row_group(row0):
    return jnp.where(row0 < N_CTX_ROWS, 0, 1 + (row0 - N_CTX_ROWS) // LAT_LEN)


def _tile_seq(tt):
    return jnp.where(tt < CTX_TILES, tt, CTX_TILES + (tt - CTX_TILES) // LAT_TILES_PER_SEQ)


def _expm1(x):
    poly = x * (1.0 + x * (0.5 + x * (1.0 / 6.0 + x * (1.0 / 24.0 + x * (1.0 / 120.0 + x * (1.0 / 720.0))))))
    return jnp.where(jnp.abs(x) < 0.1, poly, jnp.exp(x) - 1.0)


def _rms(x, g):
    return x * lax.rsqrt(jnp.mean(x * x, axis=-1, keepdims=True) + EPS) * g


def _ada_body(c_ref, w_ref, b_ref, o_ref):
    c = c_ref[...]
    s = (c * jax.nn.sigmoid(c)).astype(BF16)
    o_ref[0] = jnp.dot(s, w_ref[0].astype(BF16), preferred_element_type=F32) + b_ref[0]


def _ada(c_all, ada_w, ada_b):
    tn = 512
    nj = 6 * D_MODEL // tn
    return pl.pallas_call(
        _ada_body,
        grid=(DEPTH, nj),
        in_specs=[
            pl.BlockSpec((8, D_MODEL), lambda l, j: (0, 0)),
            pl.BlockSpec((1, D_MODEL, tn), lambda l, j: (l, 0, j)),
            pl.BlockSpec((1, 1, tn), lambda l, j: (l, 0, j)),
        ],
        out_specs=pl.BlockSpec((1, 8, tn), lambda l, j: (l, 0, j)),
        out_shape=jax.ShapeDtypeStruct((DEPTH, 8, 6 * D_MODEL), F32),
        compiler_params=_cparams(("arbitrary", "arbitrary")),
        name="ada_mod",
    )(c_all, ada_w, ada_b.reshape(DEPTH, 1, 6 * D_MODEL))


IN_TM = 1024
IN_TN = 1024


def _inproj_body(x_ref, mod_ref, g_ref, w_ref, o_ref, h_ref):
    @pl.when(pl.program_id(1) == 0)
    def _():
        y = _rms(x_ref[...], g_ref[...])
        h = y * (1.0 + mod_ref[0, 1:2, :]) + mod_ref[0, 0:1, :]
        h_ref[...] = h.astype(BF16)

    o_ref[...] = jnp.dot(h_ref[...], w_ref[...], preferred_element_type=F32)


def _inproj(x, mod, norm_g, w_in_p):
    return pl.pallas_call(
        _inproj_body,
        grid=(N_ROWS // IN_TM, P_COLS // IN_TN),
        in_specs=[
            pl.BlockSpec((IN_TM, D_MODEL), lambda i, j: (i, 0)),
            pl.BlockSpec((1, 6, D_MODEL), lambda i, j: (_row_group(i * IN_TM), 0, 0)),
            pl.BlockSpec((1, D_MODEL), lambda i, j: (0, 0)),
            pl.BlockSpec((D_MODEL, IN_TN), lambda i, j: (0, j)),
        ],
        out_specs=pl.BlockSpec((IN_TM, IN_TN), lambda i, j: (i, j)),
        out_shape=jax.ShapeDtypeStruct((N_ROWS, P_COLS), F32),
        scratch_shapes=[pltpu.VMEM((IN_TM, D_MODEL), BF16)],
        compiler_params=_cparams(("arbitrary", "arbitrary")),
        name="in_proj",
    )(x, mod, norm_g.reshape(1, D_MODEL), w_in_p)


def _scan_tile(rev, t):
    tt = (N_SEQ_TILES - 1 - t) if rev else t
    edge = (LAT_TILES_PER_SEQ - 1) if rev else 0
    first = jnp.logical_or(tt < CTX_TILES, ((tt - CTX_TILES) % LAT_TILES_PER_SEQ) == edge)
    return tt, first


def _conv_taps(rev, x, halo):
    n = x.shape[0]
    if not rev:
        ext = jnp.concatenate([halo, x], axis=0)
        return lambda k: x if k == 0 else pltpu.roll(ext, k, axis=0)[HALO:]
    ext = jnp.concatenate([x, halo], axis=0)
    return lambda k: x if k == 0 else pltpu.roll(ext, n + HALO - k, axis=0)[:n]


def _short_conv(rev, x, halo, cw):
    tap = _conv_taps(rev, x, halo)
    acc = tap(3) * cw[0:1, :]
    for j in range(1, 4):
        acc = acc + tap(3 - j) * cw[j:j + 1, :]
    return acc


def _next_halo(rev, x):
    return x[0:HALO] if rev else x[x.shape[0] - HALO:]


def _rglru_body(rev, *refs):
    if rev:
        (xr_ref, hf_ref, xg_ref, h0_ref, cw_ref, cb_ref, wa_ref, ba_ref, wx_ref, bx_ref, lam_ref,
         out_ref, st_ref, halo_ref, carry_ref) = refs
    else:
        (xr_ref, h0_ref, cw_ref, cb_ref, wa_ref, ba_ref, wx_ref, bx_ref, lam_ref,
         out_ref, st_ref, halo_ref, carry_ref) = refs
    _, first = _scan_tile(rev, pl.program_id(0))

    @pl.when(first)
    def _():
        halo_ref[...] = jnp.zeros_like(halo_ref)
        carry_ref[...] = h0_ref[0]

    x = xr_ref[...]
    xc = _short_conv(rev, x, halo_ref[...], cw_ref[...]) + cb_ref[...]
    halo_ref[...] = _next_halo(rev, x)

    xcb = xc.astype(BF16)
    rs, gs = [], []
    for hh in range(4):
        xh = xcb[:, HEAD_W * hh:HEAD_W * (hh + 1)]
        rs.append(jnp.dot(xh, wa_ref[hh], preferred_element_type=F32))
        gs.append(jnp.dot(xh, wx_ref[hh], preferred_element_type=F32))
    r = jax.nn.sigmoid(jnp.concatenate(rs, axis=1) + ba_ref[...])
    gi = jax.nn.sigmoid(jnp.concatenate(gs, axis=1) + bx_ref[...])
    log_a = (-RG_C * jax.nn.softplus(-lam_ref[...])) * r
    a = jnp.exp(log_a)
    b = jnp.sqrt(-_expm1(2.0 * log_a)) * gi * xc

    n = SEQ_TILE
    row = lax.broadcasted_iota(jnp.int32, (n, BR_W), 0)
    k = 1
    while k < n:
        if rev:
            keep = row < n - k
            a_s = jnp.where(keep, pltpu.roll(a, n - k, axis=0), 1.0)
            b_s = jnp.where(keep, pltpu.roll(b, n - k, axis=0), 0.0)
        else:
            keep = row >= k
            a_s = jnp.where(keep, pltpu.roll(a, k, axis=0), 1.0)
            b_s = jnp.where(keep, pltpu.roll(b, k, axis=0), 0.0)
        b = a * b_s + b
        a = a * a_s
        k *= 2
    h = b + a * carry_ref[...]
    last = h[0:1] if rev else h[n - 1:n]
    carry_ref[...] = last
    st_ref[0] = last
    if rev:
        out_ref[...] = (jax.nn.gelu(xg_ref[...]) * (hf_ref[...] + h)).astype(BF16)
    else:
        out_ref[...] = h


def _rglru_dir(rev, proj, hf, h0, cw, cb, wa, ba, wx, bx, lam):
    def tile_map(t):
        return (N_SEQ_TILES - 1 - t) if rev else t

    row_blk = lambda cb_: pl.BlockSpec((SEQ_TILE, COL_BLK), lambda t: (tile_map(t), cb_))
    full = lambda shp: pl.BlockSpec(shp, lambda t: (0,) * len(shp))
    in_specs = [row_blk(CB_RG_X)]
    args = [proj]
    if rev:
        in_specs += [pl.BlockSpec((SEQ_TILE, BR_W), lambda t: (tile_map(t), 0)), row_blk(CB_RG_G)]
        args += [hf, proj]
    in_specs += [
        pl.BlockSpec((1, 1, BR_W), lambda t: (_tile_seq(tile_map(t)), 0, 0)),
        full((4, BR_W)), full((1, BR_W)), full((4, HEAD_W, HEAD_W)), full((1, BR_W)),
        full((4, HEAD_W, HEAD_W)), full((1, BR_W)), full((1, BR_W)),
    ]
    args += [h0, cw, cb.reshape(1, BR_W), wa.astype(BF16), ba.reshape(1, BR_W), wx.astype(BF16),
             bx.reshape(1, BR_W), lam.reshape(1, BR_W)]
    return pl.pallas_call(
        functools.partial(_rglru_body, rev),
        grid=(N_SEQ_TILES,),
        in_specs=in_specs,
        out_specs=[
            pl.BlockSpec((SEQ_TILE, BR_W), lambda t: (tile_map(t), 0)),
            pl.BlockSpec((1, 1, BR_W), lambda t: (tile_map(t), 0, 0)),
        ],
        out_shape=[
            jax.ShapeDtypeStruct((N_ROWS, BR_W), BF16 if rev else F32),
            jax.ShapeDtypeStruct((N_SEQ_TILES, 1, BR_W), F32),
        ],
        scratch_shapes=[pltpu.VMEM((HALO, BR_W), F32), pltpu.VMEM((1, BR_W), F32)],
        compiler_params=_cparams(("arbitrary",)),
        name="rglru_bwd" if rev else "rglru_fwd",
    )(*args)


def _pool_body(x_ref, w_ref, sc_ref, o_ref):
    shift = jnp.where(pl.program_id(0) < CTX_TILES, 8, 6)
    t = lax.broadcasted_iota(jnp.int32, (SEQ_TILE, SEQ_TILE), 0)
    s = lax.broadcasted_iota(jnp.int32, (SEQ_TILE, SEQ_TILE), 1)
    same = jnp.right_shift(t, shift) == jnp.right_shift(s, shift)
    d = s - t
    x = x_ref[...]
    outs = []
    for j, w in enumerate(POOL_WINDOWS):
        inwin = jnp.logical_and(jnp.logical_and(d >= -(w // 2), d <= w - 1 - w // 2), same)
        m = jnp.where(inwin, 1.0, 0.0)
        cnt = jnp.sum(m, axis=1, keepdims=True)
        xg = x[:, HEAD_W * j:HEAD_W * (j + 1)]
        ssum = jnp.dot(m, xg, preferred_element_type=F32, precision=HIGHEST)
        pooled = ssum / cnt - xg
        outs.append(jnp.dot(pooled.astype(BF16), w_ref[j], preferred_element_type=F32))
    o_ref[...] = (jnp.concatenate(outs, axis=1) * sc_ref[...]).astype(BF16)


def _pool(proj, w_pool, scale):
    return pl.pallas_call(
        _pool_body,
        grid=(N_SEQ_TILES,),
        in_specs=[
            pl.BlockSpec((SEQ_TILE, COL_BLK), lambda t: (t, CB_POOL)),
            pl.BlockSpec((4, HEAD_W, HEAD_W), lambda t: (0, 0, 0)),
            pl.BlockSpec((1, BR_W), lambda t: (0, 0)),
        ],
        out_specs=pl.BlockSpec((SEQ_TILE, BR_W), lambda t: (t, 0)),
        out_shape=jax.ShapeDtypeStruct((N_ROWS, BR_W), BF16),
        compiler_params=_cparams(("arbitrary",)),
        name="pool_mixer",
    )(proj, w_pool.astype(BF16), scale.reshape(1, BR_W))


def _sgu_body(u_ref, v_ref, lg_ref, lb_ref, ws_ref, bst_ref, o_ref):
    u = jax.nn.gelu(u_ref[...])
    v = jax.nn.gelu(v_ref[...])
    mu = jnp.mean(v, axis=-1, keepdims=True)
    var = jnp.mean(jnp.square(v - mu), axis=-1, keepdims=True)
    vb = ((v - mu) * lax.rsqrt(var + EPS) * lg_ref[...] + lb_ref[...]).astype(BF16)
    rows = []
    for cc in range(SEQ_TILE // SGU_CHUNK):
        cols = []
        for hh in range(4):
            blk = vb[SGU_CHUNK * cc:SGU_CHUNK * (cc + 1), HEAD_W * hh:HEAD_W * (hh + 1)]
            cols.append(jnp.dot(ws_ref[hh], blk, preferred_element_type=F32) + bst_ref[:, hh:hh + 1])
        rows.append(jnp.concatenate(cols, axis=1))
    o_ref[...] = (u * jnp.concatenate(rows, axis=0)).astype(BF16)


def _sgu(proj, ln_g, ln_b, ws, bs):
    return pl.pallas_call(
        _sgu_body,
        grid=(N_SEQ_TILES,),
        in_specs=[
            pl.BlockSpec((SEQ_TILE, COL_BLK), lambda t: (t, CB_SGU_U)),
            pl.BlockSpec((SEQ_TILE, COL_BLK), lambda t: (t, CB_SGU_V)),
            pl.BlockSpec((1, BR_W), lambda t: (0, 0)),
            pl.BlockSpec((1, BR_W), lambda t: (0, 0)),
            pl.BlockSpec((4, SGU_CHUNK, SGU_CHUNK), lambda t: (0, 0, 0)),
            pl.BlockSpec((SGU_CHUNK, 4), lambda t: (0, 0)),
        ],
        out_specs=pl.BlockSpec((SEQ_TILE, BR_W), lambda t: (t, 0)),
        out_shape=jax.ShapeDtypeStruct((N_ROWS, BR_W), BF16),
        compiler_params=_cparams(("arbitrary",)),
        name="sgu_mixer",
    )(proj, proj, ln_g.reshape(1, BR_W), ln_b.reshape(1, BR_W), ws.astype(BF16), bs.T)


def _dot_nt(a, b):
    return lax.dot_general(a, b, (((1,), (1,)), ((), ())), preferred_element_type=F32)


def _dot_tn(a, b):
    return lax.dot_general(a, b, (((0,), (0,)), ((), ())), preferred_element_type=F32)


def _dot_hi(a, b):
    return jnp.dot(a, b, preferred_element_type=F32, precision=HIGHEST)


def _dot_split(a, b):
    ah = a.astype(BF16)
    al = (a - ah.astype(F32)).astype(BF16)
    bh = b.astype(BF16)
    bl = (b - bh.astype(F32)).astype(BF16)
    return (jnp.dot(ah, bh, preferred_element_type=F32) + jnp.dot(al, bh, preferred_element_type=F32)
            + jnp.dot(ah, bl, preferred_element_type=F32))


def _l2norm(x):
    return x * lax.rsqrt(jnp.sum(x * x, axis=-1, keepdims=True) + EPS)


def _dn_body(rev, *refs):
    if rev:
        (q_ref, k_ref, v_ref, ba_ref, of_ref, z_ref, s0_ref, cw_ref, alog_ref, dt_ref, ng_ref,
         out_ref, sfin_ref, halo_ref, s_ref) = refs
    else:
        (q_ref, k_ref, v_ref, ba_ref, s0_ref, cw_ref, alog_ref, dt_ref,
         out_ref, sfin_ref, halo_ref, s_ref) = refs
    _, first = _scan_tile(rev, pl.program_id(0))

    @pl.when(first)
    def _():
        halo_ref[...] = jnp.zeros_like(halo_ref)
        s_ref[...] = s0_ref[0]

    ys = []
    for p, ref in enumerate((q_ref, k_ref, v_ref)):
        x = ref[...]
        y = _short_conv(rev, x, halo_ref[p], cw_ref[:, BR_W * p:BR_W * (p + 1)])
        halo_ref[p] = _next_halo(rev, x)
        ys.append(y * jax.nn.sigmoid(y))
    yq, yk, yv = ys

    ba = ba_ref[...]
    beta_all = jax.nn.sigmoid(ba)
    g_all = -jnp.exp(alog_ref[...]) * jax.nn.softplus(ba + dt_ref[...])
    d_off = 8 if rev else 0

    T, C = SEQ_TILE, DN_CHUNK
    n_chunks = T // C
    ri = lax.broadcasted_iota(jnp.int32, (T, T), 0)
    rj = lax.broadcasted_iota(jnp.int32, (T, T), 1)
    same = jnp.right_shift(ri, 6) == jnp.right_shift(rj, 6)
    incl = jnp.logical_and(same, (ri <= rj) if rev else (ri >= rj))
    strict = jnp.logical_and(same, (ri < rj) if rev else (ri > rj))
    eye = jnp.where(ri == rj, 1.0, 0.0)
    order = range(n_chunks - 1, -1, -1) if rev else range(n_chunks)

    gsum = _dot_hi(jnp.concatenate([jnp.where(incl, 1.0, 0.0), jnp.where(same, 1.0, 0.0)], axis=0), g_all)
    gcum, gtot = gsum[:T], gsum[T:]
    gcum_t = gcum.T

    hd = []
    for hh in range(DN_HEADS):
        ls = slice(HEAD_W * hh, HEAD_W * (hh + 1))
        col = d_off + 4 + hh
        g_col = gcum[:, col:col + 1]
        g_row = gcum_t[col:col + 1, :]
        g_tot = gtot[:, col:col + 1]
        beta = beta_all[:, d_off + hh:d_off + hh + 1]
        qh = _l2norm(yq[:, ls]) * (HEAD_W ** -0.5)
        kh = _l2norm(yk[:, ls])
        decay = jnp.where(incl, jnp.exp(jnp.minimum(g_col - g_row, 0.0)), 0.0)
        eg = jnp.exp(g_col)
        kb = kh * beta
        kh16 = kh.astype(BF16)
        nmat = jnp.where(strict, _dot_nt(kb.astype(BF16), kh16) * decay, 0.0)
        hd.append(dict(
            nmat=nmat,
            rhs=jnp.concatenate([yv[:, ls] * beta, kb * eg], axis=1),
            qk=(_dot_nt(qh.astype(BF16), kh16) * decay).astype(BF16),
            qg=(qh * eg).astype(BF16),
            kend=(kh * jnp.exp(g_tot - g_col)).astype(BF16),
            gend=jnp.exp(g_tot),
        ))

    for h in hd:
        n16 = h["nmat"].astype(BF16)
        h["xinv"] = eye - h["nmat"]
        h["pw"] = jnp.dot(n16, n16, preferred_element_type=F32).astype(BF16)
    for it in range(5):
        for h in hd:
            h["xinv"] = h["xinv"] + jnp.dot(h["xinv"].astype(BF16), h["pw"], preferred_element_type=F32)
        if it < 4:
            for h in hd:
                h["pw"] = jnp.dot(h["pw"], h["pw"], preferred_element_type=F32).astype(BF16)
    for h in hd:
        h["res"] = (eye - h["xinv"]) - _dot_split(h["nmat"], h["xinv"])
    for h in hd:
        h["xinv"] = h["xinv"] + jnp.dot(h["xinv"].astype(BF16), h["res"].astype(BF16), preferred_element_type=F32)
    for h in hd:
        w = _dot_split(h["xinv"], h["rhs"])
        h["val"] = w[:, :HEAD_W]
        h["kcum"] = w[:, HEAD_W:].astype(BF16)

    ss = [s_ref[hh] for hh in range(DN_HEADS)]
    us = [[None] * n_chunks for _ in range(DN_HEADS)]
    oqs = [[None] * n_chunks for _ in range(DN_HEADS)]
    for c in order:
        rs = slice(C * c, C * (c + 1))
        for hh, h in enumerate(hd):
            s16 = ss[hh].astype(BF16)
            u16 = (h["val"][rs] - jnp.dot(h["kcum"][rs], s16, preferred_element_type=F32)).astype(BF16)
            oqs[hh][c] = jnp.dot(h["qg"][rs], s16, preferred_element_type=F32)
            us[hh][c] = u16
            ss[hh] = ss[hh] * h["gend"][C * c:C * c + 1, :] + _dot_tn(h["kend"][rs], u16)
    o_heads = []
    for hh, h in enumerate(hd):
        s_ref[hh] = ss[hh]
        sfin_ref[0, hh] = ss[hh]
        o_heads.append(jnp.concatenate(oqs[hh], axis=0)
                       + jnp.dot(h["qk"], jnp.concatenate(us[hh], axis=0), preferred_element_type=F32))

    o = jnp.concatenate(o_heads, axis=1)
    if rev:
        o = o + of_ref[...]
        z = z_ref[...]
        cols = []
        for hh in range(DN_HEADS):
            oh = o[:, HEAD_W * hh:HEAD_W * (hh + 1)]
            cols.append(oh * lax.rsqrt(jnp.mean(oh * oh, axis=-1, keepdims=True) + EPS) * ng_ref[...])
        out_ref[...] = (jnp.concatenate(cols, axis=1) * (z * jax.nn.sigmoid(z))).astype(BF16)
    else:
        out_ref[...] = o


def _dn_dir(rev, proj, of, s0, cw, alog_row, dt_row, norm_g):
    def tile_map(t):
        return (N_SEQ_TILES - 1 - t) if rev else t

    row_blk = lambda cb_: pl.BlockSpec((SEQ_TILE, COL_BLK), lambda t: (tile_map(t), cb_))
    full = lambda shp: pl.BlockSpec(shp, lambda t: (0,) * len(shp))
    in_specs = [row_blk(CB_DN_Q), row_blk(CB_DN_K), row_blk(CB_DN_V),
                pl.BlockSpec((SEQ_TILE, HEAD_W), lambda t: (tile_map(t), COL_BA // HEAD_W))]
    args = [proj, proj, proj, proj]
    if rev:
        in_specs += [pl.BlockSpec((SEQ_TILE, BR_W), lambda t: (tile_map(t), 0)), row_blk(CB_DN_Z)]
        args += [of, proj]
    in_specs += [
        pl.BlockSpec((1, DN_HEADS, HEAD_W, HEAD_W), lambda t: (_tile_seq(tile_map(t)), 0, 0, 0)),
        full((4, 3 * BR_W)), full((1, HEAD_W)), full((1, HEAD_W)),
    ]
    args += [s0, cw, alog_row, dt_row]
    if rev:
        in_specs += [full((1, HEAD_W))]
        args += [norm_g.reshape(1, HEAD_W)]
    return pl.pallas_call(
        functools.partial(_dn_body, rev),
        grid=(N_SEQ_TILES,),
        in_specs=in_specs,
        out_specs=[
            pl.BlockSpec((SEQ_TILE, BR_W), lambda t: (tile_map(t), 0)),
            pl.BlockSpec((1, DN_HEADS, HEAD_W, HEAD_W), lambda t: (jnp.minimum(tile_map(t), CTX_TILES), 0, 0, 0)),
        ],
        out_shape=[
            jax.ShapeDtypeStruct((N_ROWS, BR_W), BF16 if rev else F32),
            jax.ShapeDtypeStruct((CTX_TILES + 1, DN_HEADS, HEAD_W, HEAD_W), F32),
        ],
        scratch_shapes=[pltpu.VMEM((3, HALO, BR_W), F32), pltpu.VMEM((DN_HEADS, HEAD_W, HEAD_W), F32)],
        compiler_params=_cparams(("arbitrary",)),
        name="deltanet_bwd" if rev else "deltanet_fwd",
    )(*args)


MG_TM = 512
MG_TN = 512


def _merge_body(x_ref, mod_ref, ya_ref, yb_ref, yc_ref, yd_ref, g0_ref, g1_ref, g2_ref, g3_ref,
                wbr_ref, wout_ref, o_ref, acc_ref):
    j = pl.program_id(1)

    @pl.when(j == 0)
    def _():
        acc_ref[...] = jnp.zeros_like(acc_ref)

    m = None
    for k, (y_ref, g_ref) in enumerate(((ya_ref, g0_ref), (yb_ref, g1_ref), (yc_ref, g2_ref), (yd_ref, g3_ref))):
        br = jnp.dot(y_ref[...], wbr_ref[k], preferred_element_type=F32)
        term = jax.nn.sigmoid(g_ref[...]) * br
        m = term if m is None else m + term
    acc_ref[...] += jnp.dot(m.astype(BF16), wout_ref[...], preferred_element_type=F32)

    @pl.when(j == pl.num_programs(1) - 1)
    def _():
        o_ref[...] = x_ref[...] + mod_ref[0, 2:3, :] * acc_ref[...]


def _merge(x, mod, ys, proj, w_br, w_out):
    nj = D_MODEL // MG_TN
    y_spec = pl.BlockSpec((MG_TM, BR_W), lambda i, j: (i, 0))
    gate_spec = lambda k: pl.BlockSpec((MG_TM, MG_TN), lambda i, j: (i, CB_GATE0 + k * nj + j))
    return pl.pallas_call(
        _merge_body,
        grid=(N_ROWS // MG_TM, nj),
        in_specs=[
            pl.BlockSpec((MG_TM, D_MODEL), lambda i, j: (i, 0)),
            pl.BlockSpec((1, 6, D_MODEL), lambda i, j: (_row_group(i * MG_TM), 0, 0)),
            y_spec, y_spec, y_spec, y_spec,
            gate_spec(0), gate_spec(1), gate_spec(2), gate_spec(3),
            pl.BlockSpec((4, BR_W, MG_TN), lambda i, j: (0, 0, j)),
            pl.BlockSpec((MG_TN, D_MODEL), lambda i, j: (j, 0)),
        ],
        out_specs=pl.BlockSpec((MG_TM, D_MODEL), lambda i, j: (i, 0)),
        out_shape=jax.ShapeDtypeStruct((N_ROWS, D_MODEL), F32),
        scratch_shapes=[pltpu.VMEM((MG_TM, D_MODEL), F32)],
        compiler_params=_cparams(("arbitrary", "arbitrary")),
        name="merge_out",
    )(x, mod, *ys, proj, proj, proj, proj, w_br, w_out)


N2_TM = 512


def _norm2_body(moe, *refs):
    if moe:
        x_ref, mod_ref, g_ref, wr_ref, br_ref, h_ref, route_ref = refs
    else:
        x_ref, mod_ref, g_ref, h_ref = refs
    y = _rms(x_ref[...], g_ref[...])
    h = y * (1.0 + mod_ref[0, 4:5, :]) + mod_ref[0, 3:4, :]
    h_ref[...] = h.astype(h_ref.dtype)
    if moe:
        logits = jnp.dot(h, wr_ref[...], preferred_element_type=F32, precision=HIGHEST) + br_ref[...]
        idx = lax.broadcasted_iota(jnp.int32, logits.shape, 1)
        m1 = jnp.max(logits, axis=-1, keepdims=True)
        i1 = jnp.min(jnp.where(logits == m1, idx, N_EXPERTS), axis=-1, keepdims=True)
        rest = jnp.where(idx == i1, -jnp.inf, logits)
        m2 = jnp.max(rest, axis=-1, keepdims=True)
        i2 = jnp.min(jnp.where(rest == m2, idx, N_EXPERTS), axis=-1, keepdims=True)
        e2 = jnp.exp(m2 - m1)
        w1 = 1.0 / (1.0 + e2)
        lane = lax.broadcasted_iota(jnp.int32, (logits.shape[0], 4), 1)
        route_ref[...] = jnp.where(lane == 0, i1.astype(F32),
                                   jnp.where(lane == 1, i2.astype(F32), jnp.where(lane == 2, w1, e2 * w1)))


def _norm2(x, mod, norm_g, wr=None, br=None):
    moe = wr is not None
    in_specs = [
        pl.BlockSpec((N2_TM, D_MODEL), lambda i: (i, 0)),
        pl.BlockSpec((1, 6, D_MODEL), lambda i: (_row_group(i * N2_TM), 0, 0)),
        pl.BlockSpec((1, D_MODEL), lambda i: (0, 0)),
    ]
    args = [x, mod, norm_g.reshape(1, D_MODEL)]
    out_specs = [pl.BlockSpec((N2_TM, D_MODEL), lambda i: (i, 0))]
    out_shape = [jax.ShapeDtypeStruct((N_ROWS, D_MODEL), F32 if moe else BF16)]
    if moe:
        in_specs += [pl.BlockSpec((D_MODEL, N_EXPERTS), lambda i: (0, 0)), pl.BlockSpec((1, N_EXPERTS), lambda i: (0, 0))]
        args += [wr, br.reshape(1, N_EXPERTS)]
        out_specs += [pl.BlockSpec((N2_TM, 4), lambda i: (i, 0))]
        out_shape += [jax.ShapeDtypeStruct((N_ROWS, 4), F32)]
    return pl.pallas_call(
        functools.partial(_norm2_body, moe),
        grid=(N_ROWS // N2_TM,),
        in_specs=in_specs,
        out_specs=out_specs,
        out_shape=out_shape,
        compiler_params=_cparams(("arbitrary",)),
        name="norm2_router" if moe else "norm2",
    )(*args)


FF_TM = 512
FF_TF = 512


def _ffn_body(final, h_ref, x_ref, mod_ref, fg_ref, w1_ref, w3_ref, w2_ref, o_ref, acc_ref):
    s = pl.program_id(1)

    @pl.when(s == 0)
    def _():
        acc_ref[...] = jnp.zeros_like(acc_ref)

    h = h_ref[...]
    a1 = jnp.dot(h, w1_ref[...], preferred_element_type=F32)
    a3 = jnp.dot(h, w3_ref[...], preferred_element_type=F32)
    act = a1 * jax.nn.sigmoid(a1) * a3
    acc_ref[...] += jnp.dot(act.astype(BF16), w2_ref[...], preferred_element_type=F32)

    @pl.when(s == pl.num_programs(1) - 1)
    def _():
        xn = x_ref[...] + mod_ref[0, 5:6, :] * acc_ref[...]
        o_ref[...] = _rms(xn, fg_ref[...]) if final else xn


def _ffn(h, x, mod, final_g, w1, w3, w2, final):
    ff = w1.shape[1]
    return pl.pallas_call(
        functools.partial(_ffn_body, final),
        grid=(N_ROWS // FF_TM, ff // FF_TF),
        in_specs=[
            pl.BlockSpec((FF_TM, D_MODEL), lambda i, s: (i, 0)),
            pl.BlockSpec((FF_TM, D_MODEL), lambda i, s: (i, 0)),
            pl.BlockSpec((1, 6, D_MODEL), lambda i, s: (_row_group(i * FF_TM), 0, 0)),
            pl.BlockSpec((1, D_MODEL), lambda i, s: (0, 0)),
            pl.BlockSpec((D_MODEL, FF_TF), lambda i, s: (0, s)),
            pl.BlockSpec((D_MODEL, FF_TF), lambda i, s: (0, s)),
            pl.BlockSpec((FF_TF, D_MODEL), lambda i, s: (s, 0)),
        ],
        out_specs=pl.BlockSpec((FF_TM, D_MODEL), lambda i, s: (i, 0)),
        out_shape=jax.ShapeDtypeStruct((N_ROWS, D_MODEL), F32),
        scratch_shapes=[pltpu.VMEM((FF_TM, D_MODEL), F32)],
        compiler_params=_cparams(("arbitrary", "arbitrary")),
        name="dense_ffn",
    )(h, x, mod, final_g.reshape(1, D_MODEL), w1, w3, w2)


TOP_K = 2
N_ASSIGN = TOP_K * N_ROWS
MOE_TM = 512
MOE_TF = 256
MOE_TILES = 72
MOE_ROWS = MOE_TILES * MOE_TM
CB_TM = 256


def _route_plan(route):
    ids = route[:, :TOP_K].astype(jnp.int32).reshape(-1)
    wts = route[:, TOP_K:].reshape(-1)
    onehot = (ids[:, None] == jnp.arange(N_EXPERTS, dtype=jnp.int32)[None, :]).astype(jnp.int32)
    csum = jnp.cumsum(onehot, axis=0)
    counts = csum[-1]
    rank = jnp.take_along_axis(csum, ids[:, None], axis=1)[:, 0] - 1
    padded = ((counts + MOE_TM - 1) // MOE_TM) * MOE_TM
    end_pad = jnp.cumsum(padded)
    start_pad = end_pad - padded
    start = jnp.cumsum(counts) - counts
    pos = start_pad[ids] + rank
    order = jnp.argsort(ids, stable=True).astype(jnp.int32)
    n_used = (end_pad[-1] // MOE_TM).astype(jnp.int32).reshape(1)
    tile_start = jnp.arange(MOE_TILES, dtype=jnp.int32) * MOE_TM
    tile_expert = jnp.minimum(jnp.sum((tile_start[:, None] >= end_pad[None, :]).astype(jnp.int32), axis=1),
                              N_EXPERTS - 1).astype(jnp.int32)
    rows = jnp.arange(MOE_ROWS, dtype=jnp.int32)
    e_row = tile_expert[rows // MOE_TM]
    j = rows - start_pad[e_row]
    valid = jnp.logical_and(j < counts[e_row], rows < end_pad[-1])
    a_row = order[jnp.clip(start[e_row] + j, 0, N_ASSIGN - 1)]
    row_token = jnp.where(valid, a_row // TOP_K, 0).astype(jnp.int32)
    row_w = jnp.where(valid, wts[a_row], 0.0)
    return pos.astype(jnp.int32), row_token, row_w, tile_expert, n_used


def _gather_body(nu_ref, tok_ref, src_ref, out_ref, sem):
    i = pl.program_id(0)

    def row_copy(r, src_row):
        return pltpu.make_async_copy(src_ref.at[pl.ds(src_row, 1)], out_ref.at[pl.ds(r, 1)], sem)

    @pl.when(i < nu_ref[0])
    def _():
        def issue(r, c):
            row_copy(r, tok_ref[0, 0, r]).start()
            return c

        def drain(r, c):
            row_copy(r, 0).wait()
            return c

        lax.fori_loop(0, MOE_TM, issue, 0)
        lax.fori_loop(0, MOE_TM, drain, 0)

    @pl.when(i >= nu_ref[0])
    def _():
        out_ref[...] = jnp.zeros_like(out_ref)


def _gather_rows(n_used, row_token, src):
    return pl.pallas_call(
        _gather_body,
        grid_spec=pltpu.PrefetchScalarGridSpec(
            num_scalar_prefetch=1,
            grid=(MOE_TILES,),
            in_specs=[
                pl.BlockSpec((1, 1, MOE_TM), lambda i, nu: (i, 0, 0), memory_space=pltpu.SMEM),
                pl.BlockSpec(memory_space=pl.ANY),
            ],
            out_specs=pl.BlockSpec((MOE_TM, D_MODEL), lambda i, nu: (i, 0)),
            scratch_shapes=[pltpu.SemaphoreType.DMA],
        ),
        out_shape=jax.ShapeDtypeStruct((MOE_ROWS, D_MODEL), F32),
        compiler_params=_cparams(("arbitrary",)),
        name="moe_gather",
    )(n_used, row_token.reshape(MOE_TILES, 1, MOE_TM), src)


def _moe_ffn_body(te_ref, nu_ref, xs_ref, rw_ref, w1_ref, w3_ref, w2_ref, o_ref, h_ref):
    i = pl.program_id(0)
    f = pl.program_id(1)
    used = i < nu_ref[0]

    @pl.when(f == 0)
    def _():
        o_ref[...] = jnp.zeros_like(o_ref)
        h_ref[...] = xs_ref[...].astype(BF16)

    @pl.when(used)
    def _():
        h = h_ref[...]
        a1 = jnp.dot(h, w1_ref[0], preferred_element_type=F32)
        a3 = jnp.dot(h, w3_ref[0], preferred_element_type=F32)
        act = a1 * jax.nn.sigmoid(a1) * a3 * rw_ref[...]
        o_ref[...] += jnp.dot(act.astype(BF16), w2_ref[0], preferred_element_type=F32)


def _moe_ffn(tile_expert, n_used, xs, row_w, w1, w3, w2):
    nf = w1.shape[2] // MOE_TF

    def f_blk(i, f, nu):
        return jnp.where(i < nu[0], f, nf - 1)

    return pl.pallas_call(
        _moe_ffn_body,
        grid_spec=pltpu.PrefetchScalarGridSpec(
            num_scalar_prefetch=2,
            grid=(MOE_TILES, nf),
            in_specs=[
                pl.BlockSpec((MOE_TM, D_MODEL), lambda i, f, te, nu: (i, 0)),
                pl.BlockSpec((MOE_TM, 1), lambda i, f, te, nu: (i, 0)),
                pl.BlockSpec((1, D_MODEL, MOE_TF), lambda i, f, te, nu: (te[i], 0, f_blk(i, f, nu))),
                pl.BlockSpec((1, D_MODEL, MOE_TF), lambda i, f, te, nu: (te[i], 0, f_blk(i, f, nu))),
                pl.BlockSpec((1, MOE_TF, D_MODEL), lambda i, f, te, nu: (te[i], f_blk(i, f, nu), 0)),
            ],
            out_specs=pl.BlockSpec((MOE_TM, D_MODEL), lambda i, f, te, nu: (i, 0)),
            scratch_shapes=[pltpu.VMEM((MOE_TM, D_MODEL), BF16)],
        ),
        out_shape=jax.ShapeDtypeStruct((MOE_ROWS, D_MODEL), F32),
        compiler_params=_cparams(("arbitrary", "arbitrary")),
        name="moe_ffn",
    )(tile_expert, n_used, xs, row_w.reshape(MOE_ROWS, 1), w1, w3, w2)


def _combine_body(final, pos_ref, ys_ref, x_ref, mod_ref, fg_ref, o_ref, buf_ref, sem):
    def row_copy(r, src_row):
        return pltpu.make_async_copy(ys_ref.at[pl.ds(src_row, 1)], buf_ref.at[pl.ds(r, 1)], sem)

    def issue(r, c):
        row_copy(r, pos_ref[0, 0, r]).start()
        return c

    def drain(r, c):
        row_copy(r, 0).wait()
        return c

    lax.fori_loop(0, TOP_K * CB_TM, issue, 0)
    lax.fori_loop(0, TOP_K * CB_TM, drain, 0)
    y = buf_ref[0:CB_TM, :] + buf_ref[CB_TM:, :]
    xn = x_ref[...] + mod_ref[0, 5:6, :] * y
    o_ref[...] = _rms(xn, fg_ref[...]) if final else xn


def _combine(pos, ys, x, mod, final_g, final):
    n_tiles = N_ROWS // CB_TM
    pos_t = pos.reshape(n_tiles, CB_TM, TOP_K).transpose(0, 2, 1).reshape(n_tiles, 1, TOP_K * CB_TM)
    return pl.pallas_call(
        functools.partial(_combine_body, final),
        grid=(n_tiles,),
        in_specs=[
            pl.BlockSpec((1, 1, TOP_K * CB_TM), lambda i: (i, 0, 0), memory_space=pltpu.SMEM),
            pl.BlockSpec(memory_space=pl.ANY),
            pl.BlockSpec((CB_TM, D_MODEL), lambda i: (i, 0)),
            pl.BlockSpec((1, 6, D_MODEL), lambda i: (_row_group(i * CB_TM), 0, 0)),
            pl.BlockSpec((1, D_MODEL), lambda i: (0, 0)),
        ],
        out_specs=pl.BlockSpec((CB_TM, D_MODEL), lambda i: (i, 0)),
        out_shape=jax.ShapeDtypeStruct((N_ROWS, D_MODEL), F32),
        scratch_shapes=[pltpu.VMEM((TOP_K * CB_TM, D_MODEL), F32), pltpu.SemaphoreType.DMA],
        compiler_params=_cparams(("arbitrary",)),
        name="moe_combine",
    )(pos_t, ys, x, mod, final_g.reshape(1, D_MODEL))


def _ba_row(v):
    row = jnp.zeros((1, HEAD_W), F32)
    for d in range(2):
        row = row.at[0, d * 8 + 4:d * 8 + 8].set(v[d])
    return row


def kernel(x_prompt, x_sample, state_rglru, state_delta, c, c_ctx, ada_w, ada_b, norm1_g, norm2_g, w_in, rg_conv_w, rg_conv_b, rg_wa, rg_ba, rg_wx, rg_bx, rg_lam, pool_w, pool_scale, sgu_ln_g, sgu_ln_b, sgu_ws, sgu_bs, dn_conv_w, dn_a_log, dn_dt_bias, dn_norm_g, w_br, w_out, ffn_w1, ffn_w3, ffn_w2, moe_wr, moe_br, moe_w1, moe_w3, moe_w2, final_g):
    x = jnp.concatenate([x_prompt.reshape(N_CTX_ROWS, D_MODEL), x_sample.reshape(N_ROWS - N_CTX_ROWS, D_MODEL)], axis=0)
    c_all = jnp.concatenate([c_ctx[None, :], c, jnp.zeros((8 - 1 - N_LAT_SEQ, D_MODEL), F32)], axis=0)
    mods = _ada(c_all, ada_w, ada_b).reshape(DEPTH, 8, 6, D_MODEL)

    rg_states, dn_states = [], []
    for l in range(DEPTH):
        mod = mods[l]
        w_in_p = jnp.concatenate(
            [w_in[l][:, :COL_BA + 16], jnp.zeros((D_MODEL, COL_BLK - 16), F32), w_in[l][:, COL_BA + 16:]],
            axis=1).astype(BF16)
        proj = _inproj(x, mod, norm1_g[l], w_in_p)

        rg_h0 = [jnp.concatenate([jnp.zeros((N_CTX_SEQ, BR_W), F32), state_rglru[:, l, d]], axis=0)[:, None, :]
                 for d in range(2)]
        rg_args = lambda d: (rg_conv_w[l, d], rg_conv_b[l, d], rg_wa[l, d], rg_ba[l, d], rg_wx[l, d],
                             rg_bx[l, d], rg_lam[l, d])
        hf, st_f = _rglru_dir(False, proj, None, rg_h0[0], *rg_args(0))
        y_a, st_b = _rglru_dir(True, proj, hf, rg_h0[1], *rg_args(1))
        rg_states.append(jnp.stack([st_f[:CTX_TILES, 0], st_b[:CTX_TILES, 0]], axis=1))

        y_b = _pool(proj, pool_w[l], pool_scale[l])
        y_c = _sgu(proj, sgu_ln_g[l], sgu_ln_b[l], sgu_ws[l], sgu_bs[l])

        dn_s0 = [jnp.concatenate([jnp.zeros((N_CTX_SEQ, DN_HEADS, HEAD_W, HEAD_W), F32), state_delta[:, l, d]], axis=0)
                 for d in range(2)]
        alog_row, dt_row = _ba_row(dn_a_log[l]), _ba_row(dn_dt_bias[l])
        of, s_f = _dn_dir(False, proj, None, dn_s0[0], dn_conv_w[l, 0], alog_row, dt_row, None)
        y_d, s_b = _dn_dir(True, proj, of, dn_s0[1], dn_conv_w[l, 1], alog_row, dt_row, dn_norm_g[l])
        dn_states.append(jnp.stack([s_f[:CTX_TILES], s_b[:CTX_TILES]], axis=1))

        x = _merge(x, mod, (y_a, y_b, y_c, y_d), proj, w_br[l].astype(BF16), w_out[l].astype(BF16))

        j = l // 2
        final = l == DEPTH - 1
        if l % 2 == 0:
            (h2,) = _norm2(x, mod, norm2_g[l])
            x = _ffn(h2, x, mod, final_g, ffn_w1[j].astype(BF16), ffn_w3[j].astype(BF16),
                     ffn_w2[j].astype(BF16), final)
        else:
            h2, route = _norm2(x, mod, norm2_g[l], moe_wr[j], moe_br[j])
            pos, row_token, row_w, tile_expert, n_used = _route_plan(route)
            xs = _gather_rows(n_used, row_token, h2)
            ys = _moe_ffn(tile_expert, n_used, xs, row_w, moe_w1[j].astype(BF16), moe_w3[j].astype(BF16),
                          moe_w2[j].astype(BF16))
            x = _combine(pos, ys, x, mod, final_g, final)

    y_prompt = x[:N_CTX_ROWS].reshape(N_CTX_SEQ, CTX_LEN, D_MODEL)
    y_sample = x[N_CTX_ROWS:].reshape(N_LAT_SEQ, LAT_LEN, D_MODEL)
    return (y_prompt, y_sample, jnp.stack(rg_states, axis=1), jnp.stack(dn_states, axis=1))
```

```python
import functools

import jax
import jax.numpy as jnp
from jax import lax
from jax.experimental import pallas as pl
from jax.experimental.pallas import tpu as pltpu

F32 = jnp.float32
BF16 = jnp.bfloat16
HIGHEST = lax.Precision.HIGHEST

D_MODEL = 2048
DEPTH = 2
N_CTX_SEQ = 32
CTX_LEN = 256
N_LAT_SEQ = 2
LAT_LEN = 4096
GRID_W = 64
N_CTX_ROWS = N_CTX_SEQ * CTX_LEN
N_ROWS = N_CTX_ROWS + N_LAT_SEQ * LAT_LEN
BR_W = 512
RG_C = 8.0
POOL_WINDOWS = (2, 4, 8, 16)
SGU_CHUNK = 128
DN_CHUNK = 64
DN_HEADS = 4
HEAD_W = 128
N_EXPERTS = 8
EPS = 1e-6

SEQ_TILE = 256
N_SEQ_TILES = N_ROWS // SEQ_TILE
CTX_TILES = N_CTX_ROWS // SEQ_TILE
LAT_TILES_PER_SEQ = LAT_LEN // SEQ_TILE
HALO = 8

COL_BLK = 512
CB_RG_X, CB_RG_G, CB_POOL, CB_SGU_U, CB_SGU_V, CB_DN_Q, CB_DN_K, CB_DN_V, CB_DN_Z = range(9)
COL_BA = 4608
CB_GATE0 = 10
P_COLS = 5120 + 4 * D_MODEL

VMEM_LIMIT = 56 * 1024 * 1024


def _cparams(sem):
    return pltpu.CompilerParams(dimension_semantics=sem, vmem_limit_bytes=VMEM_LIMIT)


def _row_group(row0):
    return jnp.where(row0 < N_CTX_ROWS, 0, 1 + (row0 - N_CTX_ROWS) // LAT_LEN)


def _tile_seq(tt):
    return jnp.where(tt < CTX_TILES, tt, CTX_TILES + (tt - CTX_TILES) // LAT_TILES_PER_SEQ)


def _expm1(x):
    poly = x * (1.0 + x * (0.5 + x * (1.0 / 6.0 + x * (1.0 / 24.0 + x * (1.0 / 120.0 + x * (1.0 / 720.0))))))
    return jnp.where(jnp.abs(x) < 0.1, poly, jnp.exp(x) - 1.0)


def _rms(x, g):
    return x * lax.rsqrt(jnp.mean(x * x, axis=-1, keepdims=True) + EPS) * g


def _ada_body(c_ref, w_ref, b_ref, o_ref):
    c = c_ref[...]
    s = (c * jax.nn.sigmoid(c)).astype(BF16)
    o_ref[0] = jnp.dot(s, w_ref[0].astype(BF16), preferred_element_type=F32) + b_ref[0]


def _ada(c_all, ada_w, ada_b):
    tn = 512
    nj = 6 * D_MODEL // tn
    return pl.pallas_call(
        _ada_body,
        grid=(DEPTH, nj),
        in_specs=[
            pl.BlockSpec((8, D_MODEL), lambda l, j: (0, 0)),
            pl.BlockSpec((1, D_MODEL, tn), lambda l, j: (l, 0, j)),
            pl.BlockSpec((1, 1, tn), lambda l, j: (l, 0, j)),
        ],
        out_specs=pl.BlockSpec((1, 8, tn), lambda l, j: (l, 0, j)),
        out_shape=jax.ShapeDtypeStruct((DEPTH, 8, 6 * D_MODEL), F32),
        compiler_params=_cparams(("arbitrary", "arbitrary")),
        name="ada_mod",
    )(c_all, ada_w, ada_b.reshape(DEPTH, 1, 6 * D_MODEL))


IN_TM = 1024
IN_TN = 1024


def _inproj_body(x_ref, mod_ref, g_ref, w_ref, o_ref, h_ref):
    @pl.when(pl.program_id(1) == 0)
    def _():
        y = _rms(x_ref[...], g_ref[...])
        h = y * (1.0 + mod_ref[0, 1:2, :]) + mod_ref[0, 0:1, :]
        h_ref[...] = h.astype(BF16)

    o_ref[...] = jnp.dot(h_ref[...], w_ref[...], preferred_element_type=F32)


def _inproj(x, mod, norm_g, w_in_p):
    return pl.pallas_call(
        _inproj_body,
        grid=(N_ROWS // IN_TM, P_COLS // IN_TN),
        in_specs=[
            pl.BlockSpec((IN_TM, D_MODEL), lambda i, j: (i, 0)),
            pl.BlockSpec((1, 6, D_MODEL), lambda i, j: (_row_group(i * IN_TM), 0, 0)),
            pl.BlockSpec((1, D_MODEL), lambda i, j: (0, 0)),
            pl.BlockSpec((D_MODEL, IN_TN), lambda i, j: (0, j)),
        ],
        out_specs=pl.BlockSpec((IN_TM, IN_TN), lambda i, j: (i, j)),
        out_shape=jax.ShapeDtypeStruct((N_ROWS, P_COLS), F32),
        scratch_shapes=[pltpu.VMEM((IN_TM, D_MODEL), BF16)],
        compiler_params=_cparams(("arbitrary", "arbitrary")),
        name="in_proj",
    )(x, mod, norm_g.reshape(1, D_MODEL), w_in_p)


def _scan_tile(rev, t):
    tt = (N_SEQ_TILES - 1 - t) if rev else t
    edge = (LAT_TILES_PER_SEQ - 1) if rev else 0
    first = jnp.logical_or(tt < CTX_TILES, ((tt - CTX_TILES) % LAT_TILES_PER_SEQ) == edge)
    return tt, first


def _conv_taps(rev, x, halo):
    n = x.shape[0]
    if not rev:
        ext = jnp.concatenate([halo, x], axis=0)
        return lambda k: x if k == 0 else pltpu.roll(ext, k, axis=0)[HALO:]
    ext = jnp.concatenate([x, halo], axis=0)
    return lambda k: x if k == 0 else pltpu.roll(ext, n + HALO - k, axis=0)[:n]


def _short_conv(rev, x, halo, cw):
    tap = _conv_taps(rev, x, halo)
    acc = tap(3) * cw[0:1, :]
    for j in range(1, 4):
        acc = acc + tap(3 - j) * cw[j:j + 1, :]
    return acc


def _next_halo(rev, x):
    return x[0:HALO] if rev else x[x.shape[0] - HALO:]


def _rglru_body(rev, *refs):
    if rev:
        (xr_ref, hf_ref, xg_ref, h0_ref, cw_ref, cb_ref, wa_ref, ba_ref, wx_ref, bx_ref, lam_ref,
         out_ref, st_ref, halo_ref, carry_ref) = refs
    else:
        (xr_ref, h0_ref, cw_ref, cb_ref, wa_ref, ba_ref, wx_ref, bx_ref, lam_ref,
         out_ref, st_ref, halo_ref, carry_ref) = refs
    _, first = _scan_tile(rev, pl.program_id(0))

    @pl.when(first)
    def _():
        halo_ref[...] = jnp.zeros_like(halo_ref)
        carry_ref[...] = h0_ref[0]

    x = xr_ref[...]
    xc = _short_conv(rev, x, halo_ref[...], cw_ref[...]) + cb_ref[...]
    halo_ref[...] = _next_halo(rev, x)

    xcb = xc.astype(BF16)
    rs, gs = [], []
    for hh in range(4):
        xh = xcb[:, HEAD_W * hh:HEAD_W * (hh + 1)]
        rs.append(jnp.dot(xh, wa_ref[hh], preferred_element_type=F32))
        gs.append(jnp.dot(xh, wx_ref[hh], preferred_element_type=F32))
    r = jax.nn.sigmoid(jnp.concatenate(rs, axis=1) + ba_ref[...])
    gi = jax.nn.sigmoid(jnp.concatenate(gs, axis=1) + bx_ref[...])
    log_a = (-RG_C * jax.nn.softplus(-lam_ref[...])) * r
    a = jnp.exp(log_a)
    b = jnp.sqrt(-_expm1(2.0 * log_a)) * gi * xc

    n = SEQ_TILE
    row = lax.broadcasted_iota(jnp.int32, (n, BR_W), 0)
    k = 1
    while k < n:
        if rev:
            keep = row < n - k
            a_s = jnp.where(keep, pltpu.roll(a, n - k, axis=0), 1.0)
            b_s = jnp.where(keep, pltpu.roll(b, n - k, axis=0), 0.0)
        else:
            keep = row >= k
            a_s = jnp.where(keep, pltpu.roll(a, k, axis=0), 1.0)
            b_s = jnp.where(keep, pltpu.roll(b, k, axis=0), 0.0)
        b = a * b_s + b
        a = a * a_s
        k *= 2
    h = b + a * carry_ref[...]
    last = h[0:1] if rev else h[n - 1:n]
    carry_ref[...] = last
    st_ref[0] = last
    if rev:
        out_ref[...] = (jax.nn.gelu(xg_ref[...]) * (hf_ref[...] + h)).astype(BF16)
    else:
        out_ref[...] = h


def _rglru_dir(rev, proj, hf, h0, cw, cb, wa, ba, wx, bx, lam):
    def tile_map(t):
        return (N_SEQ_TILES - 1 - t) if rev else t

    row_blk = lambda cb_: pl.BlockSpec((SEQ_TILE, COL_BLK), lambda t: (tile_map(t), cb_))
    full = lambda shp: pl.BlockSpec(shp, lambda t: (0,) * len(shp))
    in_specs = [row_blk(CB_RG_X)]
    args = [proj]
    if rev:
        in_specs += [pl.BlockSpec((SEQ_TILE, BR_W), lambda t: (tile_map(t), 0)), row_blk(CB_RG_G)]
        args += [hf, proj]
    in_specs += [
        pl.BlockSpec((1, 1, BR_W), lambda t: (_tile_seq(tile_map(t)), 0, 0)),
        full((4, BR_W)), full((1, BR_W)), full((4, HEAD_W, HEAD_W)), full((1, BR_W)),
        full((4, HEAD_W, HEAD_W)), full((1, BR_W)), full((1, BR_W)),
    ]
    args += [h0, cw, cb.reshape(1, BR_W), wa.astype(BF16), ba.reshape(1, BR_W), wx.astype(BF16),
             bx.reshape(1, BR_W), lam.reshape(1, BR_W)]
    return pl.pallas_call(
        functools.partial(_rglru_body, rev),
        grid=(N_SEQ_TILES,),
        in_specs=in_specs,
        out_specs=[
            pl.BlockSpec((SEQ_TILE, BR_W), lambda t: (tile_map(t), 0)),
            pl.BlockSpec((1, 1, BR_W), lambda t: (tile_map(t), 0, 0)),
        ],
        out_shape=[
            jax.ShapeDtypeStruct((N_ROWS, BR_W), BF16 if rev else F32),
            jax.ShapeDtypeStruct((N_SEQ_TILES, 1, BR_W), F32),
        ],
        scratch_shapes=[pltpu.VMEM((HALO, BR_W), F32), pltpu.VMEM((1, BR_W), F32)],
        compiler_params=_cparams(("arbitrary",)),
        name="rglru_bwd" if rev else "rglru_fwd",
    )(*args)


def _pool_body(x_ref, w_ref, sc_ref, o_ref):
    shift = jnp.where(pl.program_id(0) < CTX_TILES, 8, 6)
    t = lax.broadcasted_iota(jnp.int32, (SEQ_TILE, SEQ_TILE), 0)
    s = lax.broadcasted_iota(jnp.int32, (SEQ_TILE, SEQ_TILE), 1)
    same = jnp.right_shift(t, shift) == jnp.right_shift(s, shift)
    d = s - t
    x = x_ref[...]
    outs = []
    for j, w in enumerate(POOL_WINDOWS):
        inwin = jnp.logical_and(jnp.logical_and(d >= -(w // 2), d <= w - 1 - w // 2), same)
        m = jnp.where(inwin, 1.0, 0.0)
        cnt = jnp.sum(m, axis=1, keepdims=True)
        xg = x[:, HEAD_W * j:HEAD_W * (j + 1)]
        ssum = jnp.dot(m, xg, preferred_element_type=F32, precision=HIGHEST)
        pooled = ssum / cnt - xg
        outs.append(jnp.dot(pooled.astype(BF16), w_ref[j], preferred_element_type=F32))
    o_ref[...] = (jnp.concatenate(outs, axis=1) * sc_ref[...]).astype(BF16)


def _pool(proj, w_pool, scale):
    return pl.pallas_call(
        _pool_body,
        grid=(N_SEQ_TILES,),
        in_specs=[
            pl.BlockSpec((SEQ_TILE, COL_BLK), lambda t: (t, CB_POOL)),
            pl.BlockSpec((4, HEAD_W, HEAD_W), lambda t: (0, 0, 0)),
            pl.BlockSpec((1, BR_W), lambda t: (0, 0)),
        ],
        out_specs=pl.BlockSpec((SEQ_TILE, BR_W), lambda t: (t, 0)),
        out_shape=jax.ShapeDtypeStruct((N_ROWS, BR_W), BF16),
        compiler_params=_cparams(("arbitrary",)),
        name="pool_mixer",
    )(proj, w_pool.astype(BF16), scale.reshape(1, BR_W))


def _sgu_body(u_ref, v_ref, lg_ref, lb_ref, ws_ref, bst_ref, o_ref):
    u = jax.nn.gelu(u_ref[...])
    v = jax.nn.gelu(v_ref[...])
    mu = jnp.mean(v, axis=-1, keepdims=True)
    var = jnp.mean(jnp.square(v - mu), axis=-1, keepdims=True)
    vb = ((v - mu) * lax.rsqrt(var + EPS) * lg_ref[...] + lb_ref[...]).astype(BF16)
    rows = []
    for cc in range(SEQ_TILE // SGU_CHUNK):
        cols = []
        for hh in range(4):
            blk = vb[SGU_CHUNK * cc:SGU_CHUNK * (cc + 1), HEAD_W * hh:HEAD_W * (hh + 1)]
            cols.append(jnp.dot(ws_ref[hh], blk, preferred_element_type=F32) + bst_ref[:, hh:hh + 1])
        rows.append(jnp.concatenate(cols, axis=1))
    o_ref[...] = (u * jnp.concatenate(rows, axis=0)).astype(BF16)


def _sgu(proj, ln_g, ln_b, ws, bs):
    return pl.pallas_call(
        _sgu_body,
        grid=(N_SEQ_TILES,),
        in_specs=[
            pl.BlockSpec((SEQ_TILE, COL_BLK), lambda t: (t, CB_SGU_U)),
            pl.BlockSpec((SEQ_TILE, COL_BLK), lambda t: (t, CB_SGU_V)),
            pl.BlockSpec((1, BR_W), lambda t: (0, 0)),
            pl.BlockSpec((1, BR_W), lambda t: (0, 0)),
            pl.BlockSpec((4, SGU_CHUNK, SGU_CHUNK), lambda t: (0, 0, 0)),
            pl.BlockSpec((SGU_CHUNK, 4), lambda t: (0, 0)),
        ],
        out_specs=pl.BlockSpec((SEQ_TILE, BR_W), lambda t: (t, 0)),
        out_shape=jax.ShapeDtypeStruct((N_ROWS, BR_W), BF16),
        compiler_params=_cparams(("arbitrary",)),
        name="sgu_mixer",
    )(proj, proj, ln_g.reshape(1, BR_W), ln_b.reshape(1, BR_W), ws.astype(BF16), bs.T)


def _dot_nt(a, b):
    return lax.dot_general(a, b, (((1,), (1,)), ((), ())), preferred_element_type=F32)


def _dot_tn(a, b):
    return lax.dot_general(a, b, (((0,), (0,)), ((), ())), preferred_element_type=F32)


def _dot_hi(a, b):
    return jnp.dot(a, b, preferred_element_type=F32, precision=HIGHEST)


def _dot_split(a, b):
    ah = a.astype(BF16)
    al = (a - ah.astype(F32)).astype(BF16)
    bh = b.astype(BF16)
    bl = (b - bh.astype(F32)).astype(BF16)
    return (jnp.dot(ah, bh, preferred_element_type=F32) + jnp.dot(al, bh, preferred_element_type=F32)
            + jnp.dot(ah, bl, preferred_element_type=F32))


def _l2norm(x):
    return x * lax.rsqrt(jnp.sum(x * x, axis=-1, keepdims=True) + EPS)


def _dn_body(rev, *refs):
    if rev:
        (q_ref, k_ref, v_ref, ba_ref, of_ref, z_ref, s0_ref, cw_ref, alog_ref, dt_ref, ng_ref,
         out_ref, sfin_ref, halo_ref, s_ref) = refs
    else:
        (q_ref, k_ref, v_ref, ba_ref, s0_ref, cw_ref, alog_ref, dt_ref,
         out_ref, sfin_ref, halo_ref, s_ref) = refs
    _, first = _scan_tile(rev, pl.program_id(0))

    @pl.when(first)
    def _():
        halo_ref[...] = jnp.zeros_like(halo_ref)
        s_ref[...] = s0_ref[0]

    ys = []
    for p, ref in enumerate((q_ref, k_ref, v_ref)):
        x = ref[...]
        y = _short_conv(rev, x, halo_ref[p], cw_ref[:, BR_W * p:BR_W * (p + 1)])
        halo_ref[p] = _next_halo(rev, x)
        ys.append(y * jax.nn.sigmoid(y))
    yq, yk, yv = ys

    ba = ba_ref[...]
    beta_all = jax.nn.sigmoid(ba)
    g_all = -jnp.exp(alog_ref[...]) * jax.nn.softplus(ba + dt_ref[...])
    d_off = 8 if rev else 0

    T, C = SEQ_TILE, DN_CHUNK
    n_chunks = T // C
    ri = lax.broadcasted_iota(jnp.int32, (T, T), 0)
    rj = lax.broadcasted_iota(jnp.int32, (T, T), 1)
    same = jnp.right_shift(ri, 6) == jnp.right_shift(rj, 6)
    incl = jnp.logical_and(same, (ri <= rj) if rev else (ri >= rj))
    strict = jnp.logical_and(same, (ri < rj) if rev else (ri > rj))
    eye = jnp.where(ri == rj, 1.0, 0.0)
    order = range(n_chunks - 1, -1, -1) if rev else range(n_chunks)

    gsum = _dot_hi(jnp.concatenate([jnp.where(incl, 1.0, 0.0), jnp.where(same, 1.0, 0.0)], axis=0), g_all)
    gcum, gtot = gsum[:T], gsum[T:]
    gcum_t = gcum.T

    hd = []
    for hh in range(DN_HEADS):
        ls = slice(HEAD_W * hh, HEAD_W * (hh + 1))
        col = d_off + 4 + hh
        g_col = gcum[:, col:col + 1]
        g_row = gcum_t[col:col + 1, :]
        g_tot = gtot[:, col:col + 1]
        beta = beta_all[:, d_off + hh:d_off + hh + 1]
        qh = _l2norm(yq[:, ls]) * (HEAD_W ** -0.5)
        kh = _l2norm(yk[:, ls])
        decay = jnp.where(incl, jnp.exp(jnp.minimum(g_col - g_row, 0.0)), 0.0)
        eg = jnp.exp(g_col)
        kb = kh * beta
        kh16 = kh.astype(BF16)
        nmat = jnp.where(strict, _dot_nt(kb.astype(BF16), kh16) * decay, 0.0)
        hd.append(dict(
            nmat=nmat,
            rhs=jnp.concatenate([yv[:, ls] * beta, kb * eg], axis=1),
            qk=(_dot_nt(qh.astype(BF16), kh16) * decay).astype(BF16),
            qg=(qh * eg).astype(BF16),
            kend=(kh * jnp.exp(g_tot - g_col)).astype(BF16),
            gend=jnp.exp(g_tot),
        ))

    for h in hd:
        n16 = h["nmat"].astype(BF16)
        h["xinv"] = eye - h["nmat"]
        h["pw"] = jnp.dot(n16, n16, preferred_element_type=F32).astype(BF16)
    for it in range(5):
        for h in hd:
            h["xinv"] = h["xinv"] + jnp.dot(h["xinv"].astype(BF16), h["pw"], preferred_element_type=F32)
        if it < 4:
            for h in hd:
                h["pw"] = jnp.dot(h["pw"], h["pw"], preferred_element_type=F32).astype(BF16)
    for h in hd:
        h["res"] = (eye - h["xinv"]) - _dot_split(h["nmat"], h["xinv"])
    for h in hd:
        h["xinv"] = h["xinv"] + jnp.dot(h["xinv"].astype(BF16), h["res"].astype(BF16), preferred_element_type=F32)
    for h in hd:
        w = _dot_split(h["xinv"], h["rhs"])
        h["val"] = w[:, :HEAD_W]
        h["kcum"] = w[:, HEAD_W:].astype(BF16)

    ss = [s_ref[hh] for hh in range(DN_HEADS)]
    us = [[None] * n_chunks for _ in range(DN_HEADS)]
    oqs = [[None] * n_chunks for _ in range(DN_HEADS)]
    for c in order:
        rs = slice(C * c, C * (c + 1))
        for hh, h in enumerate(hd):
            s16 = ss[hh].astype(BF16)
            u16 = (h["val"][rs] - jnp.dot(h["kcum"][rs], s16, preferred_element_type=F32)).astype(BF16)
            oqs[hh][c] = jnp.dot(h["qg"][rs], s16, preferred_element_type=F32)
            us[hh][c] = u16
            ss[hh] = ss[hh] * h["gend"][C * c:C * c + 1, :] + _dot_tn(h["kend"][rs], u16)
    o_heads = []
    for hh, h in enumerate(hd):
        s_ref[hh] = ss[hh]
        sfin_ref[0, hh] = ss[hh]
        o_heads.append(jnp.concatenate(oqs[hh], axis=0)
                       + jnp.dot(h["qk"], jnp.concatenate(us[hh], axis=0), preferred_element_type=F32))

    o = jnp.concatenate(o_heads, axis=1)
    if rev:
        o = o + of_ref[...]
        z = z_ref[...]
        cols = []
        for hh in range(DN_HEADS):
            oh = o[:, HEAD_W * hh:HEAD_W * (hh + 1)]
            cols.append(oh * lax.rsqrt(jnp.mean(oh * oh, axis=-1, keepdims=True) + EPS) * ng_ref[...])
        out_ref[...] = (jnp.concatenate(cols, axis=1) * (z * jax.nn.sigmoid(z))).astype(BF16)
    else:
        out_ref[...] = o


def _dn_dir(rev, proj, of, s0, cw, alog_row, dt_row, norm_g):
    def tile_map(t):
        return (N_SEQ_TILES - 1 - t) if rev else t

    row_blk = lambda cb_: pl.BlockSpec((SEQ_TILE, COL_BLK), lambda t: (tile_map(t), cb_))
    full = lambda shp: pl.BlockSpec(shp, lambda t: (0,) * len(shp))
    in_specs = [row_blk(CB_DN_Q), row_blk(CB_DN_K), row_blk(CB_DN_V),
                pl.BlockSpec((SEQ_TILE, HEAD_W), lambda t: (tile_map(t), COL_BA // HEAD_W))]
    args = [proj, proj, proj, proj]
    if rev:
        in_specs += [pl.BlockSpec((SEQ_TILE, BR_W), lambda t: (tile_map(t), 0)), row_blk(CB_DN_Z)]
        args += [of, proj]
    in_specs += [
        pl.BlockSpec((1, DN_HEADS, HEAD_W, HEAD_W), lambda t: (_tile_seq(tile_map(t)), 0, 0, 0)),
        full((4, 3 * BR_W)), full((1, HEAD_W)), full((1, HEAD_W)),
    ]
    args += [s0, cw, alog_row, dt_row]
    if rev:
        in_specs += [full((1, HEAD_W))]
        args += [norm_g.reshape(1, HEAD_W)]
    return pl.pallas_call(
        functools.partial(_dn_body, rev),
        grid=(N_SEQ_TILES,),
        in_specs=in_specs,
        out_specs=[
            pl.BlockSpec((SEQ_TILE, BR_W), lambda t: (tile_map(t), 0)),
            pl.BlockSpec((1, DN_HEADS, HEAD_W, HEAD_W), lambda t: (jnp.minimum(tile_map(t), CTX_TILES), 0, 0, 0)),
        ],
        out_shape=[
            jax.ShapeDtypeStruct((N_ROWS, BR_W), BF16 if rev else F32),
            jax.ShapeDtypeStruct((CTX_TILES + 1, DN_HEADS, HEAD_W, HEAD_W), F32),
        ],
        scratch_shapes=[pltpu.VMEM((3, HALO, BR_W), F32), pltpu.VMEM((DN_HEADS, HEAD_W, HEAD_W), F32)],
        compiler_params=_cparams(("arbitrary",)),
        name="deltanet_bwd" if rev else "deltanet_fwd",
    )(*args)


MG_TM = 512
MG_TN = 512


def _merge_body(x_ref, mod_ref, ya_ref, yb_ref, yc_ref, yd_ref, g0_ref, g1_ref, g2_ref, g3_ref,
                wbr_ref, wout_ref, o_ref, acc_ref):
    j = pl.program_id(1)

    @pl.when(j == 0)
    def _():
        acc_ref[...] = jnp.zeros_like(acc_ref)

    m = None
    for k, (y_ref, g_ref) in enumerate(((ya_ref, g0_ref), (yb_ref, g1_ref), (yc_ref, g2_ref), (yd_ref, g3_ref))):
        br = jnp.dot(y_ref[...], wbr_ref[k], preferred_element_type=F32)
        term = jax.nn.sigmoid(g_ref[...]) * br
        m = term if m is None else m + term
    acc_ref[...] += jnp.dot(m.astype(BF16), wout_ref[...], preferred_element_type=F32)

    @pl.when(j == pl.num_programs(1) - 1)
    def _():
        o_ref[...] = x_ref[...] + mod_ref[0, 2:3, :] * acc_ref[...]


def _merge(x, mod, ys, proj, w_br, w_out):
    nj = D_MODEL // MG_TN
    y_spec = pl.BlockSpec((MG_TM, BR_W), lambda i, j: (i, 0))
    gate_spec = lambda k: pl.BlockSpec((MG_TM, MG_TN), lambda i, j: (i, CB_GATE0 + k * nj + j))
    return pl.pallas_call(
        _merge_body,
        grid=(N_ROWS // MG_TM, nj),
        in_specs=[
            pl.BlockSpec((MG_TM, D_MODEL), lambda i, j: (i, 0)),
            pl.BlockSpec((1, 6, D_MODEL), lambda i, j: (_row_group(i * MG_TM), 0, 0)),
            y_spec, y_spec, y_spec, y_spec,
            gate_spec(0), gate_spec(1), gate_spec(2), gate_spec(3),
            pl.BlockSpec((4, BR_W, MG_TN), lambda i, j: (0, 0, j)),
            pl.BlockSpec((MG_TN, D_MODEL), lambda i, j: (j, 0)),
        ],
        out_specs=pl.BlockSpec((MG_TM, D_MODEL), lambda i, j: (i, 0)),
        out_shape=jax.ShapeDtypeStruct((N_ROWS, D_MODEL), F32),
        scratch_shapes=[pltpu.VMEM((MG_TM, D_MODEL), F32)],
        compiler_params=_cparams(("arbitrary", "arbitrary")),
        name="merge_out",
    )(x, mod, *ys, proj, proj, proj, proj, w_br, w_out)


N2_TM = 512


def _norm2_body(moe, *refs):
    if moe:
        x_ref, mod_ref, g_ref, wr_ref, br_ref, h_ref, route_ref = refs
    else:
        x_ref, mod_ref, g_ref, h_ref = refs
    y = _rms(x_ref[...], g_ref[...])
    h = y * (1.0 + mod_ref[0, 4:5, :]) + mod_ref[0, 3:4, :]
    h_ref[...] = h.astype(h_ref.dtype)
    if moe:
        logits = jnp.dot(h, wr_ref[...], preferred_element_type=F32, precision=HIGHEST) + br_ref[...]
        idx = lax.broadcasted_iota(jnp.int32, logits.shape, 1)
        m1 = jnp.max(logits, axis=-1, keepdims=True)
        i1 = jnp.min(jnp.where(logits == m1, idx, N_EXPERTS), axis=-1, keepdims=True)
        rest = jnp.where(idx == i1, -jnp.inf, logits)
        m2 = jnp.max(rest, axis=-1, keepdims=True)
        i2 = jnp.min(jnp.where(rest == m2, idx, N_EXPERTS), axis=-1, keepdims=True)
        e2 = jnp.exp(m2 - m1)
        w1 = 1.0 / (1.0 + e2)
        lane = lax.broadcasted_iota(jnp.int32, (logits.shape[0], 4), 1)
        route_ref[...] = jnp.where(lane == 0, i1.astype(F32),
                                   jnp.where(lane == 1, i2.astype(F32), jnp.where(lane == 2, w1, e2 * w1)))


def _norm2(x, mod, norm_g, wr=None, br=None):
    moe = wr is not None
    in_specs = [
        pl.BlockSpec((N2_TM, D_MODEL), lambda i: (i, 0)),
        pl.BlockSpec((1, 6, D_MODEL), lambda i: (_row_group(i * N2_TM), 0, 0)),
        pl.BlockSpec((1, D_MODEL), lambda i: (0, 0)),
    ]
    args = [x, mod, norm_g.reshape(1, D_MODEL)]
    out_specs = [pl.BlockSpec((N2_TM, D_MODEL), lambda i: (i, 0))]
    out_shape = [jax.ShapeDtypeStruct((N_ROWS, D_MODEL), F32 if moe else BF16)]
    if moe:
        in_specs += [pl.BlockSpec((D_MODEL, N_EXPERTS), lambda i: (0, 0)), pl.BlockSpec((1, N_EXPERTS), lambda i: (0, 0))]
        args += [wr, br.reshape(1, N_EXPERTS)]
        out_specs += [pl.BlockSpec((N2_TM, 4), lambda i: (i, 0))]
        out_shape += [jax.ShapeDtypeStruct((N_ROWS, 4), F32)]
    return pl.pallas_call(
        functools.partial(_norm2_body, moe),
        grid=(N_ROWS // N2_TM,),
        in_specs=in_specs,
        out_specs=out_specs,
        out_shape=out_shape,
        compiler_params=_cparams(("arbitrary",)),
        name="norm2_router" if moe else "norm2",
    )(*args)


FF_TM = 512
FF_TF = 512


def _ffn_body(final, h_ref, x_ref, mod_ref, fg_ref, w1_ref, w3_ref, w2_ref, o_ref, acc_ref):
    s = pl.program_id(1)

    @pl.when(s == 0)
    def _():
        acc_ref[...] = jnp.zeros_like(acc_ref)

    h = h_ref[...]
    a1 = jnp.dot(h, w1_ref[...], preferred_element_type=F32)
    a3 = jnp.dot(h, w3_ref[...], preferred_element_type=F32)
    act = a1 * jax.nn.sigmoid(a1) * a3
    acc_ref[...] += jnp.dot(act.astype(BF16), w2_ref[...], preferred_element_type=F32)

    @pl.when(s == pl.num_programs(1) - 1)
    def _():
        xn = x_ref[...] + mod_ref[0, 5:6, :] * acc_ref[...]
        o_ref[...] = _rms(xn, fg_ref[...]) if final else xn


def _ffn(h, x, mod, final_g, w1, w3, w2, final):
    ff = w1.shape[1]
    return pl.pallas_call(
        functools.partial(_ffn_body, final),
        grid=(N_ROWS // FF_TM, ff // FF_TF),
        in_specs=[
            pl.BlockSpec((FF_TM, D_MODEL), lambda i, s: (i, 0)),
            pl.BlockSpec((FF_TM, D_MODEL), lambda i, s: (i, 0)),
            pl.BlockSpec((1, 6, D_MODEL), lambda i, s: (_row_group(i * FF_TM), 0, 0)),
            pl.BlockSpec((1, D_MODEL), lambda i, s: (0, 0)),
            pl.BlockSpec((D_MODEL, FF_TF), lambda i, s: (0, s)),
            pl.BlockSpec((D_MODEL, FF_TF), lambda i, s: (0, s)),
            pl.BlockSpec((FF_TF, D_MODEL), lambda i, s: (s, 0)),
        ],
        out_specs=pl.BlockSpec((FF_TM, D_MODEL), lambda i, s: (i, 0)),
        out_shape=jax.ShapeDtypeStruct((N_ROWS, D_MODEL), F32),
        scratch_shapes=[pltpu.VMEM((FF_TM, D_MODEL), F32)],
        compiler_params=_cparams(("arbitrary", "arbitrary")),
        name="dense_ffn",
    )(h, x, mod, final_g.reshape(1, D_MODEL), w1, w3, w2)


TOP_K = 2
N_ASSIGN = TOP_K * N_ROWS
MOE_TM = 512
MOE_TF = 256
MOE_TILES = 72
MOE_ROWS = MOE_TILES * MOE_TM
CB_TM = 256


def _route_plan(route):
    ids = route[:, :TOP_K].astype(jnp.int32).reshape(-1)
    wts = route[:, TOP_K:].reshape(-1)
    onehot = (ids[:, None] == jnp.arange(N_EXPERTS, dtype=jnp.int32)[None, :]).astype(jnp.int32)
    csum = jnp.cumsum(onehot, axis=0)
    counts = csum[-1]
    padded = ((counts + MOE_TM - 1) // MOE_TM) * MOE_TM
    end_pad = jnp.cumsum(padded)
    start_pad = end_pad - padded
    start = jnp.cumsum(counts) - counts
    pos = jnp.sum(onehot * (csum - 1 + start_pad[None, :]), axis=1).astype(jnp.int32)
    token = jnp.arange(N_ASSIGN, dtype=jnp.int32) // TOP_K
    _, tok_sorted, w_sorted = lax.sort((ids, token, wts), num_keys=1, is_stable=True)
    n_used = (end_pad[-1] // MOE_TM).astype(jnp.int32).reshape(1)
    tile = jnp.arange(MOE_TILES, dtype=jnp.int32)
    tile_expert = jnp.minimum(jnp.sum((tile[:, None] * MOE_TM >= end_pad[None, :]).astype(jnp.int32), axis=1),
                              N_EXPERTS - 1).astype(jnp.int32)
    k_in_expert = tile - start_pad[tile_expert] // MOE_TM
    src0 = jnp.clip(start[tile_expert] + k_in_expert * MOE_TM, 0, N_ASSIGN)
    cnt = jnp.where(tile < n_used[0], jnp.clip(counts[tile_expert] - k_in_expert * MOE_TM, 0, MOE_TM), 0)
    valid = jnp.arange(MOE_TM, dtype=jnp.int32)[None, :] < cnt[:, None]
    run = lambda v: jax.vmap(lambda s0: lax.dynamic_slice(v, (s0,), (MOE_TM,)))(src0)
    row_token = jnp.where(valid, run(jnp.pad(tok_sorted, (0, MOE_TM))), 0).astype(jnp.int32)
    row_w = jnp.where(valid, run(jnp.pad(w_sorted, (0, MOE_TM))), 0.0)
    return pos, row_token.reshape(-1), row_w.reshape(-1), tile_expert, n_used


def _start_row_gather(idx_ref, n_rows, src_ref, dst_ref, sem):
    def issue(p, c):
        for u in range(4):
            r = 4 * p + u
            pltpu.make_async_copy(src_ref.at[pl.ds(idx_ref[0, 0, r], 1)], dst_ref.at[pl.ds(r, 1)], sem).start(priority=u % 2)
        return c

    lax.fori_loop(0, n_rows // 4, issue, 0)


def _wait_row_gather(n_rows, src_ref, dst_ref, sem):
    pltpu.make_async_copy(src_ref.at[pl.ds(0, n_rows)], dst_ref, sem).wait()


def _moe_ffn_body(te_ref, nu_ref, tok_ref, tok_next_ref, src_ref, rw_ref, w1_ref, w3_ref, w2_ref, o_ref,
                  gbuf_ref, h_ref, sems):
    i = pl.program_id(0)
    f = pl.program_id(1)
    n_used = nu_ref[0]
    used = i < n_used
    slot = lax.rem(i, 2)

    def start_tile(idx_ref, s):
        _start_row_gather(idx_ref, MOE_TM, src_ref, gbuf_ref.at[s], sems.at[s])

    @pl.when(jnp.logical_and(f == 0, i == 0))
    def _():
        start_tile(tok_ref, 0)

    @pl.when(f == 0)
    def _():
        o_ref[...] = jnp.zeros_like(o_ref)

        @pl.when(used)
        def _():
            _wait_row_gather(MOE_TM, src_ref, gbuf_ref.at[slot], sems.at[slot])
            h_ref[...] = gbuf_ref[slot].astype(BF16)

        @pl.when(i + 1 < n_used)
        def _():
            start_tile(tok_next_ref, 1 - slot)

    @pl.when(used)
    def _():
        h = h_ref[...]
        a1 = jnp.dot(h, w1_ref[0], preferred_element_type=F32)
        a3 = jnp.dot(h, w3_ref[0], preferred_element_type=F32)
        act = a1 * jax.nn.sigmoid(a1) * a3 * rw_ref[...]
        o_ref[...] += jnp.dot(act.astype(BF16), w2_ref[0], preferred_element_type=F32)


def _moe_ffn(tile_expert, n_used, row_token, src, row_w, w1, w3, w2):
    nf = w1.shape[2] // MOE_TF

    def f_blk(i, f, nu):
        return jnp.where(i < nu[0], f, nf - 1)

    tok = row_token.reshape(MOE_TILES, 1, MOE_TM)
    return pl.pallas_call(
        _moe_ffn_body,
        grid_spec=pltpu.PrefetchScalarGridSpec(
            num_scalar_prefetch=2,
            grid=(MOE_TILES, nf),
            in_specs=[
                pl.BlockSpec((1, 1, MOE_TM), lambda i, f, te, nu: (i, 0, 0), memory_space=pltpu.SMEM),
                pl.BlockSpec((1, 1, MOE_TM), lambda i, f, te, nu: (jnp.minimum(i + 1, MOE_TILES - 1), 0, 0),
                             memory_space=pltpu.SMEM),
                pl.BlockSpec(memory_space=pl.ANY),
                pl.BlockSpec((MOE_TM, 1), lambda i, f, te, nu: (i, 0)),
                pl.BlockSpec((1, D_MODEL, MOE_TF), lambda i, f, te, nu: (te[i], 0, f_blk(i, f, nu))),
                pl.BlockSpec((1, D_MODEL, MOE_TF), lambda i, f, te, nu: (te[i], 0, f_blk(i, f, nu))),
                pl.BlockSpec((1, MOE_TF, D_MODEL), lambda i, f, te, nu: (te[i], f_blk(i, f, nu), 0)),
            ],
            out_specs=pl.BlockSpec((MOE_TM, D_MODEL), lambda i, f, te, nu: (i, 0)),
            scratch_shapes=[pltpu.VMEM((2, MOE_TM, D_MODEL), F32), pltpu.VMEM((MOE_TM, D_MODEL), BF16),
                            pltpu.SemaphoreType.DMA((2,))],
        ),
        out_shape=jax.ShapeDtypeStruct((MOE_ROWS, D_MODEL), F32),
        compiler_params=_cparams(("arbitrary", "arbitrary")),
        name="moe_ffn",
    )(tile_expert, n_used, tok, tok, src, row_w.reshape(MOE_ROWS, 1), w1, w3, w2)


def _combine_body(final, pos_ref, pos_next_ref, ys_ref, x_ref, mod_ref, fg_ref, o_ref, buf_ref, sems):
    i = pl.program_id(0)
    slot = lax.rem(i, 2)
    n_rows = TOP_K * CB_TM

    @pl.when(i == 0)
    def _():
        _start_row_gather(pos_ref, n_rows, ys_ref, buf_ref.at[0], sems.at[0])

    @pl.when(i + 1 < pl.num_programs(0))
    def _():
        _start_row_gather(pos_next_ref, n_rows, ys_ref, buf_ref.at[1 - slot], sems.at[1 - slot])

    _wait_row_gather(n_rows, ys_ref, buf_ref.at[slot], sems.at[slot])
    y = buf_ref[slot, 0:CB_TM, :] + buf_ref[slot, CB_TM:, :]
    xn = x_ref[...] + mod_ref[0, 5:6, :] * y
    o_ref[...] = _rms(xn, fg_ref[...]) if final else xn


def _combine(pos, ys, x, mod, final_g, final):
    n_tiles = N_ROWS // CB_TM
    pos_t = pos.reshape(n_tiles, CB_TM, TOP_K).transpose(0, 2, 1).reshape(n_tiles, 1, TOP_K * CB_TM)
    return pl.pallas_call(
        functools.partial(_combine_body, final),
        grid=(n_tiles,),
        in_specs=[
            pl.BlockSpec((1, 1, TOP_K * CB_TM), lambda i: (i, 0, 0), memory_space=pltpu.SMEM),
            pl.BlockSpec((1, 1, TOP_K * CB_TM), lambda i: (jnp.minimum(i + 1, n_tiles - 1), 0, 0),
                         memory_space=pltpu.SMEM),
            pl.BlockSpec(memory_space=pl.ANY),
            pl.BlockSpec((CB_TM, D_MODEL), lambda i: (i, 0)),
            pl.BlockSpec((1, 6, D_MODEL), lambda i: (_row_group(i * CB_TM), 0, 0)),
            pl.BlockSpec((1, D_MODEL), lambda i: (0, 0)),
        ],
        out_specs=pl.BlockSpec((CB_TM, D_MODEL), lambda i: (i, 0)),
        out_shape=jax.ShapeDtypeStruct((N_ROWS, D_MODEL), F32),
        scratch_shapes=[pltpu.VMEM((2, TOP_K * CB_TM, D_MODEL), F32), pltpu.SemaphoreType.DMA((2,))],
        compiler_params=_cparams(("arbitrary",)),
        name="moe_combine",
    )(pos_t, pos_t, ys, x, mod, final_g.reshape(1, D_MODEL))


def _ba_row(v):
    row = jnp.zeros((1, HEAD_W), F32)
    for d in range(2):
        row = row.at[0, d * 8 + 4:d * 8 + 8].set(v[d])
    return row


def kernel(x_prompt, x_sample, state_rglru, state_delta, c, c_ctx, ada_w, ada_b, norm1_g, norm2_g, w_in, rg_conv_w, rg_conv_b, rg_wa, rg_ba, rg_wx, rg_bx, rg_lam, pool_w, pool_scale, sgu_ln_g, sgu_ln_b, sgu_ws, sgu_bs, dn_conv_w, dn_a_log, dn_dt_bias, dn_norm_g, w_br, w_out, ffn_w1, ffn_w3, ffn_w2, moe_wr, moe_br, moe_w1, moe_w3, moe_w2, final_g):
    x = jnp.concatenate([x_prompt.reshape(N_CTX_ROWS, D_MODEL), x_sample.reshape(N_ROWS - N_CTX_ROWS, D_MODEL)], axis=0)
    c_all = jnp.concatenate([c_ctx[None, :], c, jnp.zeros((8 - 1 - N_LAT_SEQ, D_MODEL), F32)], axis=0)
    mods = _ada(c_all, ada_w, ada_b).reshape(DEPTH, 8, 6, D_MODEL)

    rg_states, dn_states = [], []
    for l in range(DEPTH):
        mod = mods[l]
        w_in_p = jnp.concatenate(
            [w_in[l][:, :COL_BA + 16].astype(BF16), jnp.zeros((D_MODEL, COL_BLK - 16), BF16),
             w_in[l][:, COL_BA + 16:].astype(BF16)], axis=1)
        proj = _inproj(x, mod, norm1_g[l], w_in_p)

        rg_h0 = [jnp.concatenate([jnp.zeros((N_CTX_SEQ, BR_W), F32), state_rglru[:, l, d]], axis=0)[:, None, :]
                 for d in range(2)]
        rg_args = lambda d: (rg_conv_w[l, d], rg_conv_b[l, d], rg_wa[l, d], rg_ba[l, d], rg_wx[l, d],
                             rg_bx[l, d], rg_lam[l, d])
        hf, st_f = _rglru_dir(False, proj, None, rg_h0[0], *rg_args(0))
        y_a, st_b = _rglru_dir(True, proj, hf, rg_h0[1], *rg_args(1))
        rg_states.append(jnp.stack([st_f[:CTX_TILES, 0], st_b[:CTX_TILES, 0]], axis=1))

        y_b = _pool(proj, pool_w[l], pool_scale[l])
        y_c = _sgu(proj, sgu_ln_g[l], sgu_ln_b[l], sgu_ws[l], sgu_bs[l])

        dn_s0 = [jnp.concatenate([jnp.zeros((N_CTX_SEQ, DN_HEADS, HEAD_W, HEAD_W), F32), state_delta[:, l, d]], axis=0)
                 for d in range(2)]
        alog_row, dt_row = _ba_row(dn_a_log[l]), _ba_row(dn_dt_bias[l])
        of, s_f = _dn_dir(False, proj, None, dn_s0[0], dn_conv_w[l, 0], alog_row, dt_row, None)
        y_d, s_b = _dn_dir(True, proj, of, dn_s0[1], dn_conv_w[l, 1], alog_row, dt_row, dn_norm_g[l])
        dn_states.append(jnp.stack([s_f[:CTX_TILES], s_b[:CTX_TILES]], axis=1))

        x = _merge(x, mod, (y_a, y_b, y_c, y_d), proj, w_br[l].astype(BF16), w_out[l].astype(BF16))

        j = l // 2
        final = l == DEPTH - 1
        if l % 2 == 0:
            (h2,) = _norm2(x, mod, norm2_g[l])
            x = _ffn(h2, x, mod, final_g, ffn_w1[j].astype(BF16), ffn_w3[j].astype(BF16),
                     ffn_w2[j].astype(BF16), final)
        else:
            h2, route = _norm2(x, mod, norm2_g[l], moe_wr[j], moe_br[j])
            pos, row_token, row_w, tile_expert, n_used = _route_plan(route)
            ys = _moe_ffn(tile_expert, n_used, row_token, h2, row_w, moe_w1[j].astype(BF16),
                          moe_w3[j].astype(BF16), moe_w2[j].astype(BF16))
            x = _combine(pos, ys, x, mod, final_g, final)

    y_prompt = x[:N_CTX_ROWS].reshape(N_CTX_SEQ, CTX_LEN, D_MODEL)
    y_sample = x[N_CTX_ROWS:].reshape(N_LAT_SEQ, LAT_LEN, D_MODEL)
    return (y_prompt, y_sample, jnp.stack(rg_states, axis=1), jnp.stack(dn_states, axis=1))
```

```python
import functools

import jax
import jax.numpy as jnp
from jax import lax
from jax.experimental import pallas as pl
from jax.experimental.pallas import tpu as pltpu

F32 = jnp.float32
BF16 = jnp.bfloat16

D_MODEL = 2048
DEPTH = 2
N_CTX_SEQ = 32
CTX_LEN = 256
N_LAT_SEQ = 2
LAT_LEN = 4096
GRID_W = 64
N_CTX_ROWS = N_CTX_SEQ * CTX_LEN
N_ROWS = N_CTX_ROWS + N_LAT_SEQ * LAT_LEN
BR_W = 512
RG_C = 8.0
POOL_WINDOWS = (2, 4, 8, 16)
SGU_CHUNK = 128
DN_CHUNK = 64
DN_HEADS = 4
HEAD_W = 128
N_EXPERTS = 8
EPS = 1e-6

SEQ_TILE = 256
N_SEQ_TILES = N_ROWS // SEQ_TILE
CTX_TILES = N_CTX_ROWS // SEQ_TILE
LAT_TILES_PER_SEQ = LAT_LEN // SEQ_TILE
HALO = 8

COL_BLK = 512
CB_RG_X, CB_RG_G, CB_POOL, CB_SGU_U, CB_SGU_V, CB_DN_Q, CB_DN_K, CB_DN_V, CB_DN_Z = range(9)
COL_BA = 4608
CB_GATE0 = 10
P_COLS = 5120 + 4 * D_MODEL

VMEM_LIMIT = 56 * 1024 * 1024


def _cparams(sem):
    return pltpu.CompilerParams(dimension_semantics=sem, vmem_limit_bytes=VMEM_LIMIT)


def _row_group(row0):
    return jnp.where(row0 < N_CTX_ROWS, 0, 1 + (row0 - N_CTX_ROWS) // LAT_LEN)


def _tile_seq(tt):
    return jnp.where(tt < CTX_TILES, tt, CTX_TILES + (tt - CTX_TILES) // LAT_TILES_PER_SEQ)


def _expm1(x):
    poly = x * (1.0 + x * (0.5 + x * (1.0 / 6.0 + x * (1.0 / 24.0 + x * (1.0 / 120.0 + x * (1.0 / 720.0))))))
    return jnp.where(jnp.abs(x) < 0.1, poly, jnp.exp(x) - 1.0)


def _rms(x, g):
    return x * lax.rsqrt(jnp.mean(x * x, axis=-1, keepdims=True) + EPS) * g


def _split16(a):
    ah = a.astype(BF16)
    return ah, (a - ah.astype(F32)).astype(BF16)


def _dot_mask(m16, x):
    n = x.shape[1]
    xh = x.astype(BF16)
    r = x - xh.astype(F32)
    xm = r.astype(BF16)
    xl = (r - xm.astype(F32)).astype(BF16)
    y = jnp.dot(m16, jnp.concatenate([xh, xm, xl], axis=1), preferred_element_type=F32)
    return y[:, :n] + y[:, n:2 * n] + y[:, 2 * n:]


def _ada_body(c_ref, w_ref, b_ref, o_ref):
    c = c_ref[...]
    s = (c * jax.nn.sigmoid(c)).astype(BF16)
    o_ref[0] = jnp.dot(s, w_ref[0].astype(BF16), preferred_element_type=F32) + b_ref[0]


def _ada(c_all, ada_w, ada_b):
    tn = 512
    nj = 6 * D_MODEL // tn
    return pl.pallas_call(
        _ada_body,
        grid=(DEPTH, nj),
        in_specs=[
            pl.BlockSpec((8, D_MODEL), lambda l, j: (0, 0)),
            pl.BlockSpec((1, D_MODEL, tn), lambda l, j: (l, 0, j)),
            pl.BlockSpec((1, 1, tn), lambda l, j: (l, 0, j)),
        ],
        out_specs=pl.BlockSpec((1, 8, tn), lambda l, j: (l, 0, j)),
        out_shape=jax.ShapeDtypeStruct((DEPTH, 8, 6 * D_MODEL), F32),
        compiler_params=_cparams(("arbitrary", "arbitrary")),
        name="ada_mod",
    )(c_all, ada_w, ada_b.reshape(DEPTH, 1, 6 * D_MODEL))


IN_TM = 1024
IN_TN = 1024


def _inproj_body(x_ref, mod_ref, g_ref, w_ref, o_ref, h_ref):
    @pl.when(pl.program_id(1) == 0)
    def _():
        y = _rms(x_ref[...], g_ref[...])
        h = y * (1.0 + mod_ref[0, 1:2, :]) + mod_ref[0, 0:1, :]
        h_ref[...] = h.astype(BF16)

    o_ref[...] = jnp.dot(h_ref[...], w_ref[...], preferred_element_type=F32)


def _inproj(x, mod, norm_g, w_in_p):
    return pl.pallas_call(
        _inproj_body,
        grid=(N_ROWS // IN_TM, P_COLS // IN_TN),
        in_specs=[
            pl.BlockSpec((IN_TM, D_MODEL), lambda i, j: (i, 0)),
            pl.BlockSpec((1, 6, D_MODEL), lambda i, j: (_row_group(i * IN_TM), 0, 0)),
            pl.BlockSpec((1, D_MODEL), lambda i, j: (0, 0)),
            pl.BlockSpec((D_MODEL, IN_TN), lambda i, j: (0, j)),
        ],
        out_specs=pl.BlockSpec((IN_TM, IN_TN), lambda i, j: (i, j)),
        out_shape=jax.ShapeDtypeStruct((N_ROWS, P_COLS), F32),
        scratch_shapes=[pltpu.VMEM((IN_TM, D_MODEL), BF16)],
        compiler_params=_cparams(("arbitrary", "arbitrary")),
        name="in_proj",
    )(x, mod, norm_g.reshape(1, D_MODEL), w_in_p)


def _scan_tile(rev, t):
    tt = (N_SEQ_TILES - 1 - t) if rev else t
    edge = (LAT_TILES_PER_SEQ - 1) if rev else 0
    first = jnp.logical_or(tt < CTX_TILES, ((tt - CTX_TILES) % LAT_TILES_PER_SEQ) == edge)
    return tt, first


def _conv_taps(rev, x, halo):
    n = x.shape[0]
    if not rev:
        ext = jnp.concatenate([halo, x], axis=0)
        return lambda k: x if k == 0 else pltpu.roll(ext, k, axis=0)[HALO:]
    ext = jnp.concatenate([x, halo], axis=0)
    return lambda k: x if k == 0 else pltpu.roll(ext, n + HALO - k, axis=0)[:n]


def _short_conv(rev, x, halo, cw):
    tap = _conv_taps(rev, x, halo)
    acc = tap(3) * cw[0:1, :]
    for j in range(1, 4):
        acc = acc + tap(3 - j) * cw[j:j + 1, :]
    return acc


def _next_halo(rev, x):
    return x[0:HALO] if rev else x[x.shape[0] - HALO:]


def _rglru_body(rev, *refs):
    if rev:
        (xr_ref, hf_ref, xg_ref, h0_ref, cw_ref, cb_ref, wa_ref, ba_ref, wx_ref, bx_ref, lam_ref,
         out_ref, st_ref, halo_ref, carry_ref) = refs
    else:
        (xr_ref, h0_ref, cw_ref, cb_ref, wa_ref, ba_ref, wx_ref, bx_ref, lam_ref,
         out_ref, st_ref, halo_ref, carry_ref) = refs
    _, first = _scan_tile(rev, pl.program_id(0))

    @pl.when(first)
    def _():
        halo_ref[...] = jnp.zeros_like(halo_ref)
        carry_ref[...] = h0_ref[0]

    x = xr_ref[...]
    xc = _short_conv(rev, x, halo_ref[...], cw_ref[...]) + cb_ref[...]
    halo_ref[...] = _next_halo(rev, x)

    xcb = xc.astype(BF16)
    rs, gs = [], []
    for hh in range(4):
        xh = xcb[:, HEAD_W * hh:HEAD_W * (hh + 1)]
        rs.append(jnp.dot(xh, wa_ref[hh], preferred_element_type=F32))
        gs.append(jnp.dot(xh, wx_ref[hh], preferred_element_type=F32))
    r = jax.nn.sigmoid(jnp.concatenate(rs, axis=1) + ba_ref[...])
    gi = jax.nn.sigmoid(jnp.concatenate(gs, axis=1) + bx_ref[...])
    log_a = (-RG_C * jax.nn.softplus(-lam_ref[...])) * r
    a = jnp.exp(log_a)
    b = jnp.sqrt(-_expm1(2.0 * log_a)) * gi * xc

    n = SEQ_TILE
    row = lax.broadcasted_iota(jnp.int32, (n, BR_W), 0)
    k = 1
    while k < n:
        if rev:
            keep = row < n - k
            a_s = jnp.where(keep, pltpu.roll(a, n - k, axis=0), 1.0)
            b_s = jnp.where(keep, pltpu.roll(b, n - k, axis=0), 0.0)
        else:
            keep = row >= k
            a_s = jnp.where(keep, pltpu.roll(a, k, axis=0), 1.0)
            b_s = jnp.where(keep, pltpu.roll(b, k, axis=0), 0.0)
        b = a * b_s + b
        a = a * a_s
        k *= 2
    h = b + a * carry_ref[...]
    last = h[0:1] if rev else h[n - 1:n]
    carry_ref[...] = last
    st_ref[0] = last
    if rev:
        out_ref[...] = (jax.nn.gelu(xg_ref[...]) * (hf_ref[...] + h)).astype(BF16)
    else:
        out_ref[...] = h


def _rglru_dir(rev, proj, hf, h0, cw, cb, wa, ba, wx, bx, lam):
    def tile_map(t):
        return (N_SEQ_TILES - 1 - t) if rev else t

    row_blk = lambda cb_: pl.BlockSpec((SEQ_TILE, COL_BLK), lambda t: (tile_map(t), cb_))
    full = lambda shp: pl.BlockSpec(shp, lambda t: (0,) * len(shp))
    in_specs = [row_blk(CB_RG_X)]
    args = [proj]
    if rev:
        in_specs += [pl.BlockSpec((SEQ_TILE, BR_W), lambda t: (tile_map(t), 0)), row_blk(CB_RG_G)]
        args += [hf, proj]
    in_specs += [
        pl.BlockSpec((1, 1, BR_W), lambda t: (_tile_seq(tile_map(t)), 0, 0)),
        full((4, BR_W)), full((1, BR_W)), full((4, HEAD_W, HEAD_W)), full((1, BR_W)),
        full((4, HEAD_W, HEAD_W)), full((1, BR_W)), full((1, BR_W)),
    ]
    args += [h0, cw, cb.reshape(1, BR_W), wa.astype(BF16), ba.reshape(1, BR_W), wx.astype(BF16),
             bx.reshape(1, BR_W), lam.reshape(1, BR_W)]
    return pl.pallas_call(
        functools.partial(_rglru_body, rev),
        grid=(N_SEQ_TILES,),
        in_specs=in_specs,
        out_specs=[
            pl.BlockSpec((SEQ_TILE, BR_W), lambda t: (tile_map(t), 0)),
            pl.BlockSpec((1, 1, BR_W), lambda t: (tile_map(t), 0, 0)),
        ],
        out_shape=[
            jax.ShapeDtypeStruct((N_ROWS, BR_W), BF16 if rev else F32),
            jax.ShapeDtypeStruct((N_SEQ_TILES, 1, BR_W), F32),
        ],
        scratch_shapes=[pltpu.VMEM((HALO, BR_W), F32), pltpu.VMEM((1, BR_W), F32)],
        compiler_params=_cparams(("arbitrary",)),
        name="rglru_bwd" if rev else "rglru_fwd",
    )(*args)


def _pool_body(x_ref, w_ref, sc_ref, o_ref):
    shift = jnp.where(pl.program_id(0) < CTX_TILES, 8, 6)
    t = lax.broadcasted_iota(jnp.int32, (SEQ_TILE, SEQ_TILE), 0)
    s = lax.broadcasted_iota(jnp.int32, (SEQ_TILE, SEQ_TILE), 1)
    same = jnp.right_shift(t, shift) == jnp.right_shift(s, shift)
    d = s - t
    x = x_ref[...]
    outs = []
    for j, w in enumerate(POOL_WINDOWS):
        inwin = jnp.logical_and(jnp.logical_and(d >= -(w // 2), d <= w - 1 - w // 2), same)
        m = jnp.where(inwin, 1.0, 0.0)
        cnt = jnp.sum(m, axis=1, keepdims=True)
        xg = x[:, HEAD_W * j:HEAD_W * (j + 1)]
        ssum = _dot_mask(m.astype(BF16), xg)
        pooled = ssum / cnt - xg
        outs.append(jnp.dot(pooled.astype(BF16), w_ref[j], preferred_element_type=F32))
    o_ref[...] = (jnp.concatenate(outs, axis=1) * sc_ref[...]).astype(BF16)


def _pool(proj, w_pool, scale):
    return pl.pallas_call(
        _pool_body,
        grid=(N_SEQ_TILES,),
        in_specs=[
            pl.BlockSpec((SEQ_TILE, COL_BLK), lambda t: (t, CB_POOL)),
            pl.BlockSpec((4, HEAD_W, HEAD_W), lambda t: (0, 0, 0)),
            pl.BlockSpec((1, BR_W), lambda t: (0, 0)),
        ],
        out_specs=pl.BlockSpec((SEQ_TILE, BR_W), lambda t: (t, 0)),
        out_shape=jax.ShapeDtypeStruct((N_ROWS, BR_W), BF16),
        compiler_params=_cparams(("arbitrary",)),
        name="pool_mixer",
    )(proj, w_pool.astype(BF16), scale.reshape(1, BR_W))


def _sgu_body(u_ref, v_ref, lg_ref, lb_ref, ws_ref, bst_ref, o_ref):
    u = jax.nn.gelu(u_ref[...])
    v = jax.nn.gelu(v_ref[...])
    mu = jnp.mean(v, axis=-1, keepdims=True)
    var = jnp.mean(jnp.square(v - mu), axis=-1, keepdims=True)
    vb = ((v - mu) * lax.rsqrt(var + EPS) * lg_ref[...] + lb_ref[...]).astype(BF16)
    rows = []
    for cc in range(SEQ_TILE // SGU_CHUNK):
        cols = []
        for hh in range(4):
            blk = vb[SGU_CHUNK * cc:SGU_CHUNK * (cc + 1), HEAD_W * hh:HEAD_W * (hh + 1)]
            cols.append(jnp.dot(ws_ref[hh], blk, preferred_element_type=F32) + bst_ref[:, hh:hh + 1])
        rows.append(jnp.concatenate(cols, axis=1))
    o_ref[...] = (u * jnp.concatenate(rows, axis=0)).astype(BF16)


def _sgu(proj, ln_g, ln_b, ws, bs):
    return pl.pallas_call(
        _sgu_body,
        grid=(N_SEQ_TILES,),
        in_specs=[
            pl.BlockSpec((SEQ_TILE, COL_BLK), lambda t: (t, CB_SGU_U)),
            pl.BlockSpec((SEQ_TILE, COL_BLK), lambda t: (t, CB_SGU_V)),
            pl.BlockSpec((1, BR_W), lambda t: (0, 0)),
            pl.BlockSpec((1, BR_W), lambda t: (0, 0)),
            pl.BlockSpec((4, SGU_CHUNK, SGU_CHUNK), lambda t: (0, 0, 0)),
            pl.BlockSpec((SGU_CHUNK, 4), lambda t: (0, 0)),
        ],
        out_specs=pl.BlockSpec((SEQ_TILE, BR_W), lambda t: (t, 0)),
        out_shape=jax.ShapeDtypeStruct((N_ROWS, BR_W), BF16),
        compiler_params=_cparams(("arbitrary",)),
        name="sgu_mixer",
    )(proj, proj, ln_g.reshape(1, BR_W), ln_b.reshape(1, BR_W), ws.astype(BF16), bs.T)


def _dot_nt(a, b):
    return lax.dot_general(a, b, (((1,), (1,)), ((), ())), preferred_element_type=F32)


def _dot_tn(a, b):
    return lax.dot_general(a, b, (((0,), (0,)), ((), ())), preferred_element_type=F32)


def _dot_split(a, b):
    m = a.shape[0]
    ah, al = _split16(a)
    bh, bl = _split16(b)
    top = jnp.dot(jnp.concatenate([ah, al], axis=0), bh, preferred_element_type=F32)
    return top[:m] + top[m:] + jnp.dot(ah, bl, preferred_element_type=F32)


def _l2norm(x):
    return x * lax.rsqrt(jnp.sum(x * x, axis=-1, keepdims=True) + EPS)


def _dn_body(rev, *refs):
    if rev:
        (q_ref, k_ref, v_ref, ba_ref, of_ref, z_ref, s0_ref, cw_ref, alog_ref, dt_ref, ng_ref,
         out_ref, sfin_ref, halo_ref, s_ref) = refs
    else:
        (q_ref, k_ref, v_ref, ba_ref, s0_ref, cw_ref, alog_ref, dt_ref,
         out_ref, sfin_ref, halo_ref, s_ref) = refs
    _, first = _scan_tile(rev, pl.program_id(0))

    @pl.when(first)
    def _():
        halo_ref[...] = jnp.zeros_like(halo_ref)
        s_ref[...] = s0_ref[0]

    ys = []
    for p, ref in enumerate((q_ref, k_ref, v_ref)):
        x = ref[...]
        y = _short_conv(rev, x, halo_ref[p], cw_ref[:, BR_W * p:BR_W * (p + 1)])
        halo_ref[p] = _next_halo(rev, x)
        ys.append(y * jax.nn.sigmoid(y))
    yq, yk, yv = ys

    ba = ba_ref[...]
    beta_all = jax.nn.sigmoid(ba)
    g_all = -jnp.exp(alog_ref[...]) * jax.nn.softplus(ba + dt_ref[...])
    d_off = 8 if rev else 0

    T, C = SEQ_TILE, DN_CHUNK
    n_chunks = T // C
    ri = lax.broadcasted_iota(jnp.int32, (T, T), 0)
    rj = lax.broadcasted_iota(jnp.int32, (T, T), 1)
    same = jnp.right_shift(ri, 6) == jnp.right_shift(rj, 6)
    incl = jnp.logical_and(same, (ri <= rj) if rev else (ri >= rj))
    strict = jnp.logical_and(same, (ri < rj) if rev else (ri > rj))
    eye = jnp.where(ri == rj, 1.0, 0.0)
    order = range(n_chunks - 1, -1, -1) if rev else range(n_chunks)

    cum_mat = jnp.concatenate([jnp.where(incl, 1.0, 0.0), jnp.where(same, 1.0, 0.0)], axis=0).astype(BF16)
    gsum = _dot_mask(cum_mat, g_all)
    gcum, gtot = gsum[:T], gsum[T:]
    gcum_t = gcum.T

    hd = []
    for hh in range(DN_HEADS):
        ls = slice(HEAD_W * hh, HEAD_W * (hh + 1))
        col = d_off + 4 + hh
        g_col = gcum[:, col:col + 1]
        g_row = gcum_t[col:col + 1, :]
        g_tot = gtot[:, col:col + 1]
        beta = beta_all[:, d_off + hh:d_off + hh + 1]
        qh = _l2norm(yq[:, ls]) * (HEAD_W ** -0.5)
        kh = _l2norm(yk[:, ls])
        decay = jnp.where(incl, jnp.exp(jnp.minimum(g_col - g_row, 0.0)), 0.0)
        eg = jnp.exp(g_col)
        kb = kh * beta
        kh16 = kh.astype(BF16)
        kq = _dot_nt(jnp.concatenate([kb.astype(BF16), qh.astype(BF16)], axis=0), kh16)
        nmat = jnp.where(strict, kq[:T] * decay, 0.0)
        hd.append(dict(
            nmat=nmat,
            rhs=jnp.concatenate([yv[:, ls] * beta, kb * eg], axis=1),
            qk=(kq[T:] * decay).astype(BF16),
            qg=(qh * eg).astype(BF16),
            kend=(kh * jnp.exp(g_tot - g_col)).astype(BF16),
            gend=jnp.exp(g_tot),
        ))

    for h in hd:
        n16 = h["nmat"].astype(BF16)
        h["xinv"] = eye - h["nmat"]
        h["pw"] = jnp.dot(n16, n16, preferred_element_type=F32).astype(BF16)
    for it in range(5):
        for h in hd:
            if it < 4:
                xp = jnp.dot(jnp.concatenate([h["xinv"].astype(BF16), h["pw"]], axis=0), h["pw"],
                             preferred_element_type=F32)
                h["xinv"] = h["xinv"] + xp[:T]
                h["pw"] = xp[T:].astype(BF16)
            else:
                h["xinv"] = h["xinv"] + jnp.dot(h["xinv"].astype(BF16), h["pw"], preferred_element_type=F32)
    for h in hd:
        h["res"] = (eye - h["xinv"]) - _dot_split(h["nmat"], h["xinv"])
    for h in hd:
        h["xinv"] = h["xinv"] + jnp.dot(h["xinv"].astype(BF16), h["res"].astype(BF16), preferred_element_type=F32)
    for h in hd:
        w = _dot_split(h["xinv"], h["rhs"])
        h["val"] = w[:, :HEAD_W]
        h["kcum"] = w[:, HEAD_W:].astype(BF16)

    ss = [s_ref[hh] for hh in range(DN_HEADS)]
    us = [[None] * n_chunks for _ in range(DN_HEADS)]
    oqs = [[None] * n_chunks for _ in range(DN_HEADS)]
    for c in order:
        rs = slice(C * c, C * (c + 1))
        for hh, h in enumerate(hd):
            ks = jnp.dot(jnp.concatenate([h["kcum"][rs], h["qg"][rs]], axis=0), ss[hh].astype(BF16),
                         preferred_element_type=F32)
            u16 = (h["val"][rs] - ks[:C]).astype(BF16)
            oqs[hh][c] = ks[C:]
            us[hh][c] = u16
            ss[hh] = ss[hh] * h["gend"][C * c:C * c + 1, :] + _dot_tn(h["kend"][rs], u16)
    o_heads = []
    for hh, h in enumerate(hd):
        s_ref[hh] = ss[hh]
        sfin_ref[0, hh] = ss[hh]
        o_heads.append(jnp.concatenate(oqs[hh], axis=0)
                       + jnp.dot(h["qk"], jnp.concatenate(us[hh], axis=0), preferred_element_type=F32))

    o = jnp.concatenate(o_heads, axis=1)
    if rev:
        o = o + of_ref[...]
        z = z_ref[...]
        cols = []
        for hh in range(DN_HEADS):
            oh = o[:, HEAD_W * hh:HEAD_W * (hh + 1)]
            cols.append(oh * lax.rsqrt(jnp.mean(oh * oh, axis=-1, keepdims=True) + EPS) * ng_ref[...])
        out_ref[...] = (jnp.concatenate(cols, axis=1) * (z * jax.nn.sigmoid(z))).astype(BF16)
    else:
        out_ref[...] = o


def _dn_dir(rev, proj, of, s0, cw, alog_row, dt_row, norm_g):
    def tile_map(t):
        return (N_SEQ_TILES - 1 - t) if rev else t

    row_blk = lambda cb_: pl.BlockSpec((SEQ_TILE, COL_BLK), lambda t: (tile_map(t), cb_))
    full = lambda shp: pl.BlockSpec(shp, lambda t: (0,) * len(shp))
    in_specs = [row_blk(CB_DN_Q), row_blk(CB_DN_K), row_blk(CB_DN_V),
                pl.BlockSpec((SEQ_TILE, HEAD_W), lambda t: (tile_map(t), COL_BA // HEAD_W))]
    args = [proj, proj, proj, proj]
    if rev:
        in_specs += [pl.BlockSpec((SEQ_TILE, BR_W), lambda t: (tile_map(t), 0)), row_blk(CB_DN_Z)]
        args += [of, proj]
    in_specs += [
        pl.BlockSpec((1, DN_HEADS, HEAD_W, HEAD_W), lambda t: (_tile_seq(tile_map(t)), 0, 0, 0)),
        full((4, 3 * BR_W)), full((1, HEAD_W)), full((1, HEAD_W)),
    ]
    args += [s0, cw, alog_row, dt_row]
    if rev:
        in_specs += [full((1, HEAD_W))]
        args += [norm_g.reshape(1, HEAD_W)]
    return pl.pallas_call(
        functools.partial(_dn_body, rev),
        grid=(N_SEQ_TILES,),
        in_specs=in_specs,
        out_specs=[
            pl.BlockSpec((SEQ_TILE, BR_W), lambda t: (tile_map(t), 0)),
            pl.BlockSpec((1, DN_HEADS, HEAD_W, HEAD_W), lambda t: (jnp.minimum(tile_map(t), CTX_TILES), 0, 0, 0)),
        ],
        out_shape=[
            jax.ShapeDtypeStruct((N_ROWS, BR_W), BF16 if rev else F32),
            jax.ShapeDtypeStruct((CTX_TILES + 1, DN_HEADS, HEAD_W, HEAD_W), F32),
        ],
        scratch_shapes=[pltpu.VMEM((3, HALO, BR_W), F32), pltpu.VMEM((DN_HEADS, HEAD_W, HEAD_W), F32)],
        compiler_params=_cparams(("arbitrary",)),
        name="deltanet_bwd" if rev else "deltanet_fwd",
    )(*args)


MG_TM = 512
MG_TN = 512


def _merge_body(x_ref, mod_ref, ya_ref, yb_ref, yc_ref, yd_ref, g0_ref, g1_ref, g2_ref, g3_ref,
                wbr_ref, wout_ref, o_ref, acc_ref):
    j = pl.program_id(1)

    @pl.when(j == 0)
    def _():
        acc_ref[...] = jnp.zeros_like(acc_ref)

    m = None
    for k, (y_ref, g_ref) in enumerate(((ya_ref, g0_ref), (yb_ref, g1_ref), (yc_ref, g2_ref), (yd_ref, g3_ref))):
        br = jnp.dot(y_ref[...], wbr_ref[k], preferred_element_type=F32)
        term = jax.nn.sigmoid(g_ref[...]) * br
        m = term if m is None else m + term
    acc_ref[...] += jnp.dot(m.astype(BF16), wout_ref[...], preferred_element_type=F32)

    @pl.when(j == pl.num_programs(1) - 1)
    def _():
        o_ref[...] = x_ref[...] + mod_ref[0, 2:3, :] * acc_ref[...]


def _merge(x, mod, ys, proj, w_br, w_out):
    nj = D_MODEL // MG_TN
    y_spec = pl.BlockSpec((MG_TM, BR_W), lambda i, j: (i, 0))
    gate_spec = lambda k: pl.BlockSpec((MG_TM, MG_TN), lambda i, j: (i, CB_GATE0 + k * nj + j))
    return pl.pallas_call(
        _merge_body,
        grid=(N_ROWS // MG_TM, nj),
        in_specs=[
            pl.BlockSpec((MG_TM, D_MODEL), lambda i, j: (i, 0)),
            pl.BlockSpec((1, 6, D_MODEL), lambda i, j: (_row_group(i * MG_TM), 0, 0)),
            y_spec, y_spec, y_spec, y_spec,
            gate_spec(0), gate_spec(1), gate_spec(2), gate_spec(3),
            pl.BlockSpec((4, BR_W, MG_TN), lambda i, j: (0, 0, j)),
            pl.BlockSpec((MG_TN, D_MODEL), lambda i, j: (j, 0)),
        ],
        out_specs=pl.BlockSpec((MG_TM, D_MODEL), lambda i, j: (i, 0)),
        out_shape=jax.ShapeDtypeStruct((N_ROWS, D_MODEL), F32),
        scratch_shapes=[pltpu.VMEM((MG_TM, D_MODEL), F32)],
        compiler_params=_cparams(("arbitrary", "arbitrary")),
        name="merge_out",
    )(x, mod, *ys, proj, proj, proj, proj, w_br, w_out)


N2_TM = 512


def _norm2_body(moe, *refs):
    if moe:
        x_ref, mod_ref, g_ref, wr_ref, br_ref, h_ref, route_ref = refs
    else:
        x_ref, mod_ref, g_ref, h_ref = refs
    y = _rms(x_ref[...], g_ref[...])
    h = y * (1.0 + mod_ref[0, 4:5, :]) + mod_ref[0, 3:4, :]
    h_ref[...] = h.astype(h_ref.dtype)
    if moe:
        logits = _dot_split(h, wr_ref[...]) + br_ref[...]
        idx = lax.broadcasted_iota(jnp.int32, logits.shape, 1)
        m1 = jnp.max(logits, axis=-1, keepdims=True)
        i1 = jnp.min(jnp.where(logits == m1, idx, N_EXPERTS), axis=-1, keepdims=True)
        rest = jnp.where(idx == i1, -jnp.inf, logits)
        m2 = jnp.max(rest, axis=-1, keepdims=True)
        i2 = jnp.min(jnp.where(rest == m2, idx, N_EXPERTS), axis=-1, keepdims=True)
        e2 = jnp.exp(m2 - m1)
        w1 = 1.0 / (1.0 + e2)
        lane = lax.broadcasted_iota(jnp.int32, (logits.shape[0], 4), 1)
        route_ref[...] = jnp.where(lane == 0, i1.astype(F32),
                                   jnp.where(lane == 1, i2.astype(F32), jnp.where(lane == 2, w1, e2 * w1)))


def _norm2(x, mod, norm_g, wr=None, br=None):
    moe = wr is not None
    in_specs = [
        pl.BlockSpec((N2_TM, D_MODEL), lambda i: (i, 0)),
        pl.BlockSpec((1, 6, D_MODEL), lambda i: (_row_group(i * N2_TM), 0, 0)),
        pl.BlockSpec((1, D_MODEL), lambda i: (0, 0)),
    ]
    args = [x, mod, norm_g.reshape(1, D_MODEL)]
    out_specs = [pl.BlockSpec((N2_TM, D_MODEL), lambda i: (i, 0))]
    out_shape = [jax.ShapeDtypeStruct((N_ROWS, D_MODEL), F32 if moe else BF16)]
    if moe:
        in_specs += [pl.BlockSpec((D_MODEL, N_EXPERTS), lambda i: (0, 0)), pl.BlockSpec((1, N_EXPERTS), lambda i: (0, 0))]
        args += [wr, br.reshape(1, N_EXPERTS)]
        out_specs += [pl.BlockSpec((N2_TM, 4), lambda i: (i, 0))]
        out_shape += [jax.ShapeDtypeStruct((N_ROWS, 4), F32)]
    return pl.pallas_call(
        functools.partial(_norm2_body, moe),
        grid=(N_ROWS // N2_TM,),
        in_specs=in_specs,
        out_specs=out_specs,
        out_shape=out_shape,
        compiler_params=_cparams(("arbitrary",)),
        name="norm2_router" if moe else "norm2",
    )(*args)


FF_TM = 512
FF_TF = 512


def _ffn_body(final, h_ref, x_ref, mod_ref, fg_ref, w1_ref, w3_ref, w2_ref, o_ref, acc_ref):
    s = pl.program_id(1)

    @pl.when(s == 0)
    def _():
        acc_ref[...] = jnp.zeros_like(acc_ref)

    h = h_ref[...]
    a1 = jnp.dot(h, w1_ref[...], preferred_element_type=F32)
    a3 = jnp.dot(h, w3_ref[...], preferred_element_type=F32)
    act = a1 * jax.nn.sigmoid(a1) * a3
    acc_ref[...] += jnp.dot(act.astype(BF16), w2_ref[...], preferred_element_type=F32)

    @pl.when(s == pl.num_programs(1) - 1)
    def _():
        xn = x_ref[...] + mod_ref[0, 5:6, :] * acc_ref[...]
        o_ref[...] = _rms(xn, fg_ref[...]) if final else xn


def _ffn(h, x, mod, final_g, w1, w3, w2, final):
    ff = w1.shape[1]
    return pl.pallas_call(
        functools.partial(_ffn_body, final),
        grid=(N_ROWS // FF_TM, ff // FF_TF),
        in_specs=[
            pl.BlockSpec((FF_TM, D_MODEL), lambda i, s: (i, 0)),
            pl.BlockSpec((FF_TM, D_MODEL), lambda i, s: (i, 0)),
            pl.BlockSpec((1, 6, D_MODEL), lambda i, s: (_row_group(i * FF_TM), 0, 0)),
            pl.BlockSpec((1, D_MODEL), lambda i, s: (0, 0)),
            pl.BlockSpec((D_MODEL, FF_TF), lambda i, s: (0, s)),
            pl.BlockSpec((D_MODEL, FF_TF), lambda i, s: (0, s)),
            pl.BlockSpec((FF_TF, D_MODEL), lambda i, s: (s, 0)),
        ],
        out_specs=pl.BlockSpec((FF_TM, D_MODEL), lambda i, s: (i, 0)),
        out_shape=jax.ShapeDtypeStruct((N_ROWS, D_MODEL), F32),
        scratch_shapes=[pltpu.VMEM((FF_TM, D_MODEL), F32)],
        compiler_params=_cparams(("arbitrary", "arbitrary")),
        name="dense_ffn",
    )(h, x, mod, final_g.reshape(1, D_MODEL), w1, w3, w2)


TOP_K = 2
N_ASSIGN = TOP_K * N_ROWS
MOE_TM = 512
MOE_TF = 256
MOE_TILES = 72
MOE_ROWS = MOE_TILES * MOE_TM
CB_TM = 256


def _route_plan(route):
    ids = route[:, :TOP_K].astype(jnp.int32).reshape(-1)
    onehot = (ids[:, None] == jnp.arange(N_EXPERTS, dtype=jnp.int32)[None, :]).astype(jnp.int32)
    csum = jnp.cumsum(onehot, axis=0)
    counts = csum[-1]
    padded = ((counts + MOE_TM - 1) // MOE_TM) * MOE_TM
    end_pad = jnp.cumsum(padded)
    start_pad = end_pad - padded
    start = jnp.cumsum(counts) - counts
    pos = jnp.sum(onehot * (csum - 1 + start_pad[None, :]), axis=1).astype(jnp.int32)
    token = jnp.arange(N_ASSIGN, dtype=jnp.int32) // TOP_K
    _, tok_sorted = lax.sort((ids, token), num_keys=1, is_stable=True)
    n_used = (end_pad[-1] // MOE_TM).astype(jnp.int32).reshape(1)
    tile = jnp.arange(MOE_TILES, dtype=jnp.int32)
    tile_expert = jnp.minimum(jnp.sum((tile[:, None] * MOE_TM >= end_pad[None, :]).astype(jnp.int32), axis=1),
                              N_EXPERTS - 1).astype(jnp.int32)
    k_in_expert = tile - start_pad[tile_expert] // MOE_TM
    src0 = jnp.clip(start[tile_expert] + k_in_expert * MOE_TM, 0, N_ASSIGN)
    cnt = jnp.where(tile < n_used[0], jnp.clip(counts[tile_expert] - k_in_expert * MOE_TM, 0, MOE_TM), 0)
    valid = jnp.arange(MOE_TM, dtype=jnp.int32)[None, :] < cnt[:, None]
    run = lambda v: jax.vmap(lambda s0: lax.dynamic_slice(v, (s0,), (MOE_TM,)))(src0)
    row_token = jnp.where(valid, run(jnp.pad(tok_sorted, (0, MOE_TM))), 0).astype(jnp.int32)
    return pos, row_token.reshape(-1), tile_expert, n_used


def _start_row_gather(idx_ref, n_rows, src_ref, dst_ref, sem):
    def issue(p, c):
        for u in range(4):
            r = 4 * p + u
            pltpu.make_async_copy(src_ref.at[pl.ds(idx_ref[0, 0, r], 1)], dst_ref.at[pl.ds(r, 1)], sem).start(priority=u % 2)
        return c

    lax.fori_loop(0, n_rows // 4, issue, 0)


def _wait_row_gather(n_rows, src_ref, dst_ref, sem):
    pltpu.make_async_copy(src_ref.at[pl.ds(0, n_rows)], dst_ref, sem).wait()


def _moe_ffn_body(te_ref, nu_ref, tok_ref, tok_next_ref, src_ref, w1_ref, w3_ref, w2_ref, o_ref,
                  gbuf_ref, h_ref, sems):
    i = pl.program_id(0)
    f = pl.program_id(1)
    n_used = nu_ref[0]
    used = i < n_used
    slot = lax.rem(i, 2)

    def start_tile(idx_ref, s):
        _start_row_gather(idx_ref, MOE_TM, src_ref, gbuf_ref.at[s], sems.at[s])

    @pl.when(jnp.logical_and(f == 0, i == 0))
    def _():
        start_tile(tok_ref, 0)

    @pl.when(f == 0)
    def _():
        o_ref[...] = jnp.zeros_like(o_ref)

        @pl.when(used)
        def _():
            _wait_row_gather(MOE_TM, src_ref, gbuf_ref.at[slot], sems.at[slot])
            h_ref[...] = gbuf_ref[slot].astype(BF16)

        @pl.when(i + 1 < n_used)
        def _():
            start_tile(tok_next_ref, 1 - slot)

    @pl.when(used)
    def _():
        h = h_ref[...]
        a1 = jnp.dot(h, w1_ref[0], preferred_element_type=F32)
        a3 = jnp.dot(h, w3_ref[0], preferred_element_type=F32)
        act = a1 * jax.nn.sigmoid(a1) * a3
        o_ref[...] += jnp.dot(act.astype(BF16), w2_ref[0], preferred_element_type=F32)


def _moe_ffn(tile_expert, n_used, row_token, src, w1, w3, w2):
    nf = w1.shape[2] // MOE_TF

    def f_blk(i, f, nu):
        return jnp.where(i < nu[0], f, nf - 1)

    tok = row_token.reshape(MOE_TILES, 1, MOE_TM)
    return pl.pallas_call(
        _moe_ffn_body,
        grid_spec=pltpu.PrefetchScalarGridSpec(
            num_scalar_prefetch=2,
            grid=(MOE_TILES, nf),
            in_specs=[
                pl.BlockSpec((1, 1, MOE_TM), lambda i, f, te, nu: (i, 0, 0), memory_space=pltpu.SMEM),
                pl.BlockSpec((1, 1, MOE_TM), lambda i, f, te, nu: (jnp.minimum(i + 1, MOE_TILES - 1), 0, 0),
                             memory_space=pltpu.SMEM),
                pl.BlockSpec(memory_space=pl.ANY),
                pl.BlockSpec((1, D_MODEL, MOE_TF), lambda i, f, te, nu: (te[i], 0, f_blk(i, f, nu))),
                pl.BlockSpec((1, D_MODEL, MOE_TF), lambda i, f, te, nu: (te[i], 0, f_blk(i, f, nu))),
                pl.BlockSpec((1, MOE_TF, D_MODEL), lambda i, f, te, nu: (te[i], f_blk(i, f, nu), 0)),
            ],
            out_specs=pl.BlockSpec((MOE_TM, D_MODEL), lambda i, f, te, nu: (i, 0)),
            scratch_shapes=[pltpu.VMEM((2, MOE_TM, D_MODEL), F32), pltpu.VMEM((MOE_TM, D_MODEL), BF16),
                            pltpu.SemaphoreType.DMA((2,))],
        ),
        out_shape=jax.ShapeDtypeStruct((MOE_ROWS, D_MODEL), F32),
        compiler_params=_cparams(("arbitrary", "arbitrary")),
        name="moe_ffn",
    )(tile_expert, n_used, tok, tok, src, w1, w3, w2)


def _combine_body(final, pos_ref, pos_next_ref, ys_ref, route_ref, x_ref, mod_ref, fg_ref, o_ref, buf_ref, sems):
    i = pl.program_id(0)
    slot = lax.rem(i, 2)
    n_rows = TOP_K * CB_TM

    @pl.when(i == 0)
    def _():
        _start_row_gather(pos_ref, n_rows, ys_ref, buf_ref.at[0], sems.at[0])

    @pl.when(i + 1 < pl.num_programs(0))
    def _():
        _start_row_gather(pos_next_ref, n_rows, ys_ref, buf_ref.at[1 - slot], sems.at[1 - slot])

    _wait_row_gather(n_rows, ys_ref, buf_ref.at[slot], sems.at[slot])
    route = route_ref[...]
    y = route[:, 2:3] * buf_ref[slot, 0:CB_TM, :] + route[:, 3:4] * buf_ref[slot, CB_TM:, :]
    xn = x_ref[...] + mod_ref[0, 5:6, :] * y
    o_ref[...] = _rms(xn, fg_ref[...]) if final else xn


def _combine(pos, ys, route, x, mod, final_g, final):
    n_tiles = N_ROWS // CB_TM
    pos_t = pos.reshape(n_tiles, CB_TM, TOP_K).transpose(0, 2, 1).reshape(n_tiles, 1, TOP_K * CB_TM)
    return pl.pallas_call(
        functools.partial(_combine_body, final),
        grid=(n_tiles,),
        in_specs=[
            pl.BlockSpec((1, 1, TOP_K * CB_TM), lambda i: (i, 0, 0), memory_space=pltpu.SMEM),
            pl.BlockSpec((1, 1, TOP_K * CB_TM), lambda i: (jnp.minimum(i + 1, n_tiles - 1), 0, 0),
                         memory_space=pltpu.SMEM),
            pl.BlockSpec(memory_space=pl.ANY),
            pl.BlockSpec((CB_TM, 4), lambda i: (i, 0)),
            pl.BlockSpec((CB_TM, D_MODEL), lambda i: (i, 0)),
            pl.BlockSpec((1, 6, D_MODEL), lambda i: (_row_group(i * CB_TM), 0, 0)),
            pl.BlockSpec((1, D_MODEL), lambda i: (0, 0)),
        ],
        out_specs=pl.BlockSpec((CB_TM, D_MODEL), lambda i: (i, 0)),
        out_shape=jax.ShapeDtypeStruct((N_ROWS, D_MODEL), F32),
        scratch_shapes=[pltpu.VMEM((2, TOP_K * CB_TM, D_MODEL), F32), pltpu.SemaphoreType.DMA((2,))],
        compiler_params=_cparams(("arbitrary",)),
        name="moe_combine",
    )(pos_t, pos_t, ys, route, x, mod, final_g.reshape(1, D_MODEL))


def _ba_row(v):
    row = jnp.zeros((1, HEAD_W), F32)
    for d in range(2):
        row = row.at[0, d * 8 + 4:d * 8 + 8].set(v[d])
    return row


def kernel(x_prompt, x_sample, state_rglru, state_delta, c, c_ctx, ada_w, ada_b, norm1_g, norm2_g, w_in, rg_conv_w, rg_conv_b, rg_wa, rg_ba, rg_wx, rg_bx, rg_lam, pool_w, pool_scale, sgu_ln_g, sgu_ln_b, sgu_ws, sgu_bs, dn_conv_w, dn_a_log, dn_dt_bias, dn_norm_g, w_br, w_out, ffn_w1, ffn_w3, ffn_w2, moe_wr, moe_br, moe_w1, moe_w3, moe_w2, final_g):
    x = jnp.concatenate([x_prompt.reshape(N_CTX_ROWS, D_MODEL), x_sample.reshape(N_ROWS - N_CTX_ROWS, D_MODEL)], axis=0)
    c_all = jnp.concatenate([c_ctx[None, :], c, jnp.zeros((8 - 1 - N_LAT_SEQ, D_MODEL), F32)], axis=0)
    mods = _ada(c_all, ada_w, ada_b).reshape(DEPTH, 8, 6, D_MODEL)

    rg_states, dn_states = [], []
    for l in range(DEPTH):
        mod = mods[l]
        w_in_p = jnp.concatenate(
            [w_in[l][:, :COL_BA + 16].astype(BF16), jnp.zeros((D_MODEL, COL_BLK - 16), BF16),
             w_in[l][:, COL_BA + 16:].astype(BF16)], axis=1)
        proj = _inproj(x, mod, norm1_g[l], w_in_p)

        rg_h0 = [jnp.concatenate([jnp.zeros((N_CTX_SEQ, BR_W), F32), state_rglru[:, l, d]], axis=0)[:, None, :]
                 for d in range(2)]
        rg_args = lambda d: (rg_conv_w[l, d], rg_conv_b[l, d], rg_wa[l, d], rg_ba[l, d], rg_wx[l, d],
                             rg_bx[l, d], rg_lam[l, d])
        hf, st_f = _rglru_dir(False, proj, None, rg_h0[0], *rg_args(0))
        y_a, st_b = _rglru_dir(True, proj, hf, rg_h0[1], *rg_args(1))
        rg_states.append(jnp.stack([st_f[:CTX_TILES, 0], st_b[:CTX_TILES, 0]], axis=1))

        y_b = _pool(proj, pool_w[l], pool_scale[l])
        y_c = _sgu(proj, sgu_ln_g[l], sgu_ln_b[l], sgu_ws[l], sgu_bs[l])

        dn_s0 = [jnp.concatenate([jnp.zeros((N_CTX_SEQ, DN_HEADS, HEAD_W, HEAD_W), F32), state_delta[:, l, d]], axis=0)
                 for d in range(2)]
        alog_row, dt_row = _ba_row(dn_a_log[l]), _ba_row(dn_dt_bias[l])
        of, s_f = _dn_dir(False, proj, None, dn_s0[0], dn_conv_w[l, 0], alog_row, dt_row, None)
        y_d, s_b = _dn_dir(True, proj, of, dn_s0[1], dn_conv_w[l, 1], alog_row, dt_row, dn_norm_g[l])
        dn_states.append(jnp.stack([s_f[:CTX_TILES], s_b[:CTX_TILES]], axis=1))

        x = _merge(x, mod, (y_a, y_b, y_c, y_d), proj, w_br[l].astype(BF16), w_out[l].astype(BF16))

        j = l // 2
        final = l == DEPTH - 1
        if l % 2 == 0:
            (h2,) = _norm2(x, mod, norm2_g[l])
            x = _ffn(h2, x, mod, final_g, ffn_w1[j].astype(BF16), ffn_w3[j].astype(BF16),
                     ffn_w2[j].astype(BF16), final)
        else:
            h2, route = _norm2(x, mod, norm2_g[l], moe_wr[j], moe_br[j])
            pos, row_token, tile_expert, n_used = _route_plan(route)
            ys = _moe_ffn(tile_expert, n_used, row_token, h2, moe_w1[j].astype(BF16), moe_w3[j].astype(BF16),
                          moe_w2[j].astype(BF16))
            x = _combine(pos, ys, route, x, mod, final_g, final)

    y_prompt = x[:N_CTX_ROWS].reshape(N_CTX_SEQ, CTX_LEN, D_MODEL)
    y_sample = x[N_CTX_ROWS:].reshape(N_LAT_SEQ, LAT_LEN, D_MODEL)
    return (y_prompt, y_sample, jnp.stack(rg_states, axis=1), jnp.stack(dn_states, axis=1))
```

```python
import functools

import jax
import jax.numpy as jnp
from jax import lax
from jax.experimental import pallas as pl
from jax.experimental.pallas import tpu as pltpu

F32 = jnp.float32
BF16 = jnp.bfloat16

D_MODEL = 2048
DEPTH = 2
N_CTX_SEQ = 32
CTX_LEN = 256
N_LAT_SEQ = 2
LAT_LEN = 4096
GRID_W = 64
N_CTX_ROWS = N_CTX_SEQ * CTX_LEN
N_ROWS = N_CTX_ROWS + N_LAT_SEQ * LAT_LEN
BR_W = 512
RG_C = 8.0
POOL_WINDOWS = (2, 4, 8, 16)
SGU_CHUNK = 128
DN_CHUNK = 64
DN_HEADS = 4
HEAD_W = 128
N_EXPERTS = 8
EPS = 1e-6

SEQ_TILE = 256
N_SEQ_TILES = N_ROWS // SEQ_TILE
CTX_TILES = N_CTX_ROWS // SEQ_TILE
LAT_TILES_PER_SEQ = LAT_LEN // SEQ_TILE
HALO = 8

COL_BLK = 512
CB_RG_X, CB_RG_G, CB_POOL, CB_SGU_U, CB_SGU_V, CB_DN_Q, CB_DN_K, CB_DN_V, CB_DN_Z = range(9)
COL_BA = 4608
CB_GATE0 = 10
P_COLS = 5120 + 4 * D_MODEL

VMEM_LIMIT = 56 * 1024 * 1024


def _cparams(sem):
    return pltpu.CompilerParams(dimension_semantics=sem, vmem_limit_bytes=VMEM_LIMIT)


def _row_group(row0):
    return jnp.where(row0 < N_CTX_ROWS, 0, 1 + (row0 - N_CTX_ROWS) // LAT_LEN)


def _tile_seq(tt):
    return jnp.where(tt < CTX_TILES, tt, CTX_TILES + (tt - CTX_TILES) // LAT_TILES_PER_SEQ)


def _expm1(x):
    poly = x * (1.0 + x * (0.5 + x * (1.0 / 6.0 + x * (1.0 / 24.0 + x * (1.0 / 120.0 + x * (1.0 / 720.0))))))
    return jnp.where(jnp.abs(x) < 0.1, poly, jnp.exp(x) - 1.0)


def _rms(x, g):
    return x * lax.rsqrt(jnp.mean(x * x, axis=-1, keepdims=True) + EPS) * g


def _split16(a):
    ah = a.astype(BF16)
    return ah, (a - ah.astype(F32)).astype(BF16)


def _dot_mask(m16, x):
    n = x.shape[1]
    xh = x.astype(BF16)
    r = x - xh.astype(F32)
    xm = r.astype(BF16)
    xl = (r - xm.astype(F32)).astype(BF16)
    y = jnp.dot(m16, jnp.concatenate([xh, xm, xl], axis=1), preferred_element_type=F32)
    return y[:, :n] + y[:, n:2 * n] + y[:, 2 * n:]


def _ada_body(c_ref, w_ref, b_ref, o_ref):
    c = c_ref[...]
    s = (c * jax.nn.sigmoid(c)).astype(BF16)
    o_ref[0] = jnp.dot(s, w_ref[0].astype(BF16), preferred_element_type=F32) + b_ref[0]


def _ada(c_all, ada_w, ada_b):
    tn = 512
    nj = 6 * D_MODEL // tn
    return pl.pallas_call(
        _ada_body,
        grid=(DEPTH, nj),
        in_specs=[
            pl.BlockSpec((8, D_MODEL), lambda l, j: (0, 0)),
            pl.BlockSpec((1, D_MODEL, tn), lambda l, j: (l, 0, j)),
            pl.BlockSpec((1, 1, tn), lambda l, j: (l, 0, j)),
        ],
        out_specs=pl.BlockSpec((1, 8, tn), lambda l, j: (l, 0, j)),
        out_shape=jax.ShapeDtypeStruct((DEPTH, 8, 6 * D_MODEL), F32),
        compiler_params=_cparams(("arbitrary", "arbitrary")),
        name="ada_mod",
    )(c_all, ada_w, ada_b.reshape(DEPTH, 1, 6 * D_MODEL))


IN_TM = 1024
IN_TN = 1024


IN_NA = (CB_GATE0 * COL_BLK) // IN_TN


def _inproj_body(x_ref, mod_ref, g_ref, wa_ref, wg_ref, o_ref, h_ref):
    j = pl.program_id(1)

    @pl.when(j == 0)
    def _():
        y = _rms(x_ref[...], g_ref[...])
        h = y * (1.0 + mod_ref[0, 1:2, :]) + mod_ref[0, 0:1, :]
        h_ref[...] = h.astype(BF16)

    @pl.when(j < IN_NA)
    def _():
        o_ref[...] = jnp.dot(h_ref[...], wa_ref[...], preferred_element_type=F32)

    @pl.when(j >= IN_NA)
    def _():
        o_ref[...] = jnp.dot(h_ref[...], wg_ref[...], preferred_element_type=F32)


def _inproj(x, mod, norm_g, w_mix, w_gate):
    return pl.pallas_call(
        _inproj_body,
        grid=(N_ROWS // IN_TM, P_COLS // IN_TN),
        in_specs=[
            pl.BlockSpec((IN_TM, D_MODEL), lambda i, j: (i, 0)),
            pl.BlockSpec((1, 6, D_MODEL), lambda i, j: (_row_group(i * IN_TM), 0, 0)),
            pl.BlockSpec((1, D_MODEL), lambda i, j: (0, 0)),
            pl.BlockSpec((D_MODEL, IN_TN), lambda i, j: (0, jnp.minimum(j, IN_NA - 1))),
            pl.BlockSpec((D_MODEL, IN_TN), lambda i, j: (0, jnp.maximum(j - IN_NA, 0))),
        ],
        out_specs=pl.BlockSpec((IN_TM, IN_TN), lambda i, j: (i, j)),
        out_shape=jax.ShapeDtypeStruct((N_ROWS, P_COLS), F32),
        scratch_shapes=[pltpu.VMEM((IN_TM, D_MODEL), BF16)],
        compiler_params=_cparams(("arbitrary", "arbitrary")),
        name="in_proj",
    )(x, mod, norm_g.reshape(1, D_MODEL), w_mix, w_gate)


def _scan_tile(rev, t):
    tt = (N_SEQ_TILES - 1 - t) if rev else t
    edge = (LAT_TILES_PER_SEQ - 1) if rev else 0
    first = jnp.logical_or(tt < CTX_TILES, ((tt - CTX_TILES) % LAT_TILES_PER_SEQ) == edge)
    return tt, first


def _conv_taps(rev, x, halo):
    n = x.shape[0]
    if not rev:
        ext = jnp.concatenate([halo, x], axis=0)
        return lambda k: x if k == 0 else pltpu.roll(ext, k, axis=0)[HALO:]
    ext = jnp.concatenate([x, halo], axis=0)
    return lambda k: x if k == 0 else pltpu.roll(ext, n + HALO - k, axis=0)[:n]


def _short_conv(rev, x, halo, cw):
    tap = _conv_taps(rev, x, halo)
    acc = tap(3) * cw[0:1, :]
    for j in range(1, 4):
        acc = acc + tap(3 - j) * cw[j:j + 1, :]
    return acc


def _next_halo(rev, x):
    return x[0:HALO] if rev else x[x.shape[0] - HALO:]


def _rglru_body(rev, *refs):
    if rev:
        (xr_ref, hf_ref, xg_ref, h0_ref, cw_ref, cb_ref, wa_ref, ba_ref, wx_ref, bx_ref, lam_ref,
         out_ref, st_ref, halo_ref, carry_ref) = refs
    else:
        (xr_ref, h0_ref, cw_ref, cb_ref, wa_ref, ba_ref, wx_ref, bx_ref, lam_ref,
         out_ref, st_ref, halo_ref, carry_ref) = refs
    _, first = _scan_tile(rev, pl.program_id(0))

    @pl.when(first)
    def _():
        halo_ref[...] = jnp.zeros_like(halo_ref)
        carry_ref[...] = h0_ref[0]

    x = xr_ref[...]
    xc = _short_conv(rev, x, halo_ref[...], cw_ref[...]) + cb_ref[...]
    halo_ref[...] = _next_halo(rev, x)

    xcb = xc.astype(BF16)
    rs, gs = [], []
    for hh in range(4):
        xh = xcb[:, HEAD_W * hh:HEAD_W * (hh + 1)]
        rs.append(jnp.dot(xh, wa_ref[hh], preferred_element_type=F32))
        gs.append(jnp.dot(xh, wx_ref[hh], preferred_element_type=F32))
    r = jax.nn.sigmoid(jnp.concatenate(rs, axis=1) + ba_ref[...])
    gi = jax.nn.sigmoid(jnp.concatenate(gs, axis=1) + bx_ref[...])
    log_a = (-RG_C * jax.nn.softplus(-lam_ref[...])) * r
    a = jnp.exp(log_a)
    b = jnp.sqrt(-_expm1(2.0 * log_a)) * gi * xc

    n = SEQ_TILE
    row = lax.broadcasted_iota(jnp.int32, (n, BR_W), 0)
    k = 1
    while k < n:
        if rev:
            keep = row < n - k
            a_s = jnp.where(keep, pltpu.roll(a, n - k, axis=0), 1.0)
            b_s = jnp.where(keep, pltpu.roll(b, n - k, axis=0), 0.0)
        else:
            keep = row >= k
            a_s = jnp.where(keep, pltpu.roll(a, k, axis=0), 1.0)
            b_s = jnp.where(keep, pltpu.roll(b, k, axis=0), 0.0)
        b = a * b_s + b
        a = a * a_s
        k *= 2
    h = b + a * carry_ref[...]
    last = h[0:1] if rev else h[n - 1:n]
    carry_ref[...] = last
    st_ref[0] = last
    if rev:
        out_ref[...] = (jax.nn.gelu(xg_ref[...]) * (hf_ref[...] + h)).astype(BF16)
    else:
        out_ref[...] = h


def _rglru_dir(rev, proj, hf, h0, cw, cb, wa, ba, wx, bx, lam):
    def tile_map(t):
        return (N_SEQ_TILES - 1 - t) if rev else t

    row_blk = lambda cb_: pl.BlockSpec((SEQ_TILE, COL_BLK), lambda t: (tile_map(t), cb_))
    full = lambda shp: pl.BlockSpec(shp, lambda t: (0,) * len(shp))
    in_specs = [row_blk(CB_RG_X)]
    args = [proj]
    if rev:
        in_specs += [pl.BlockSpec((SEQ_TILE, BR_W), lambda t: (tile_map(t), 0)), row_blk(CB_RG_G)]
        args += [hf, proj]
    in_specs += [
        pl.BlockSpec((1, 1, BR_W), lambda t: (_tile_seq(tile_map(t)), 0, 0)),
        full((4, BR_W)), full((1, BR_W)), full((4, HEAD_W, HEAD_W)), full((1, BR_W)),
        full((4, HEAD_W, HEAD_W)), full((1, BR_W)), full((1, BR_W)),
    ]
    args += [h0, cw, cb.reshape(1, BR_W), wa.astype(BF16), ba.reshape(1, BR_W), wx.astype(BF16),
             bx.reshape(1, BR_W), lam.reshape(1, BR_W)]
    return pl.pallas_call(
        functools.partial(_rglru_body, rev),
        grid=(N_SEQ_TILES,),
        in_specs=in_specs,
        out_specs=[
            pl.BlockSpec((SEQ_TILE, BR_W), lambda t: (tile_map(t), 0)),
            pl.BlockSpec((1, 1, BR_W), lambda t: (tile_map(t), 0, 0)),
        ],
        out_shape=[
            jax.ShapeDtypeStruct((N_ROWS, BR_W), BF16 if rev else F32),
            jax.ShapeDtypeStruct((N_SEQ_TILES, 1, BR_W), F32),
        ],
        scratch_shapes=[pltpu.VMEM((HALO, BR_W), F32), pltpu.VMEM((1, BR_W), F32)],
        compiler_params=_cparams(("arbitrary",)),
        name="rglru_bwd" if rev else "rglru_fwd",
    )(*args)


def _pool_body(x_ref, w_ref, sc_ref, o_ref):
    shift = jnp.where(pl.program_id(0) < CTX_TILES, 8, 6)
    t = lax.broadcasted_iota(jnp.int32, (SEQ_TILE, SEQ_TILE), 0)
    s = lax.broadcasted_iota(jnp.int32, (SEQ_TILE, SEQ_TILE), 1)
    same = jnp.right_shift(t, shift) == jnp.right_shift(s, shift)
    d = s - t
    x = x_ref[...]
    outs = []
    for j, w in enumerate(POOL_WINDOWS):
        inwin = jnp.logical_and(jnp.logical_and(d >= -(w // 2), d <= w - 1 - w // 2), same)
        m = jnp.where(inwin, 1.0, 0.0)
        cnt = jnp.sum(m, axis=1, keepdims=True)
        xg = x[:, HEAD_W * j:HEAD_W * (j + 1)]
        ssum = _dot_mask(m.astype(BF16), xg)
        pooled = ssum / cnt - xg
        outs.append(jnp.dot(pooled.astype(BF16), w_ref[j], preferred_element_type=F32))
    o_ref[...] = (jnp.concatenate(outs, axis=1) * sc_ref[...]).astype(BF16)


def _pool(proj, w_pool, scale):
    return pl.pallas_call(
        _pool_body,
        grid=(N_SEQ_TILES,),
        in_specs=[
            pl.BlockSpec((SEQ_TILE, COL_BLK), lambda t: (t, CB_POOL)),
            pl.BlockSpec((4, HEAD_W, HEAD_W), lambda t: (0, 0, 0)),
            pl.BlockSpec((1, BR_W), lambda t: (0, 0)),
        ],
        out_specs=pl.BlockSpec((SEQ_TILE, BR_W), lambda t: (t, 0)),
        out_shape=jax.ShapeDtypeStruct((N_ROWS, BR_W), BF16),
        compiler_params=_cparams(("arbitrary",)),
        name="pool_mixer",
    )(proj, w_pool.astype(BF16), scale.reshape(1, BR_W))


def _sgu_body(u_ref, v_ref, lg_ref, lb_ref, ws_ref, bst_ref, o_ref):
    u = jax.nn.gelu(u_ref[...])
    v = jax.nn.gelu(v_ref[...])
    mu = jnp.mean(v, axis=-1, keepdims=True)
    var = jnp.mean(jnp.square(v - mu), axis=-1, keepdims=True)
    vb = ((v - mu) * lax.rsqrt(var + EPS) * lg_ref[...] + lb_ref[...]).astype(BF16)
    rows = []
    for cc in range(SEQ_TILE // SGU_CHUNK):
        cols = []
        for hh in range(4):
            blk = vb[SGU_CHUNK * cc:SGU_CHUNK * (cc + 1), HEAD_W * hh:HEAD_W * (hh + 1)]
            cols.append(jnp.dot(ws_ref[hh], blk, preferred_element_type=F32) + bst_ref[:, hh:hh + 1])
        rows.append(jnp.concatenate(cols, axis=1))
    o_ref[...] = (u * jnp.concatenate(rows, axis=0)).astype(BF16)


def _sgu(proj, ln_g, ln_b, ws, bs):
    return pl.pallas_call(
        _sgu_body,
        grid=(N_SEQ_TILES,),
        in_specs=[
            pl.BlockSpec((SEQ_TILE, COL_BLK), lambda t: (t, CB_SGU_U)),
            pl.BlockSpec((SEQ_TILE, COL_BLK), lambda t: (t, CB_SGU_V)),
            pl.BlockSpec((1, BR_W), lambda t: (0, 0)),
            pl.BlockSpec((1, BR_W), lambda t: (0, 0)),
            pl.BlockSpec((4, SGU_CHUNK, SGU_CHUNK), lambda t: (0, 0, 0)),
            pl.BlockSpec((SGU_CHUNK, 4), lambda t: (0, 0)),
        ],
        out_specs=pl.BlockSpec((SEQ_TILE, BR_W), lambda t: (t, 0)),
        out_shape=jax.ShapeDtypeStruct((N_ROWS, BR_W), BF16),
        compiler_params=_cparams(("arbitrary",)),
        name="sgu_mixer",
    )(proj, proj, ln_g.reshape(1, BR_W), ln_b.reshape(1, BR_W), ws.astype(BF16), bs.T)


def _dot_nt(a, b):
    return lax.dot_general(a, b, (((1,), (1,)), ((), ())), preferred_element_type=F32)


def _dot_tn(a, b):
    return lax.dot_general(a, b, (((0,), (0,)), ((), ())), preferred_element_type=F32)


def _dot_split(a, b):
    m = a.shape[0]
    ah, al = _split16(a)
    bh, bl = _split16(b)
    top = jnp.dot(jnp.concatenate([ah, al], axis=0), bh, preferred_element_type=F32)
    return top[:m] + top[m:] + jnp.dot(ah, bl, preferred_element_type=F32)


def _l2norm(x):
    return x * lax.rsqrt(jnp.sum(x * x, axis=-1, keepdims=True) + EPS)


def _dn_body(rev, *refs):
    if rev:
        (q_ref, k_ref, v_ref, ba_ref, of_ref, z_ref, s0_ref, cw_ref, alog_ref, dt_ref, ng_ref,
         out_ref, sfin_ref, halo_ref, s_ref) = refs
    else:
        (q_ref, k_ref, v_ref, ba_ref, s0_ref, cw_ref, alog_ref, dt_ref,
         out_ref, sfin_ref, halo_ref, s_ref) = refs
    _, first = _scan_tile(rev, pl.program_id(0))

    @pl.when(first)
    def _():
        halo_ref[...] = jnp.zeros_like(halo_ref)
        s_ref[...] = s0_ref[0]

    ys = []
    for p, ref in enumerate((q_ref, k_ref, v_ref)):
        x = ref[...]
        y = _short_conv(rev, x, halo_ref[p], cw_ref[:, BR_W * p:BR_W * (p + 1)])
        halo_ref[p] = _next_halo(rev, x)
        ys.append(y * jax.nn.sigmoid(y))
    yq, yk, yv = ys

    ba = ba_ref[...]
    beta_all = jax.nn.sigmoid(ba)
    g_all = -jnp.exp(alog_ref[...]) * jax.nn.softplus(ba + dt_ref[...])
    d_off = 8 if rev else 0

    T, C = SEQ_TILE, DN_CHUNK
    n_chunks = T // C
    ri = lax.broadcasted_iota(jnp.int32, (T, T), 0)
    rj = lax.broadcasted_iota(jnp.int32, (T, T), 1)
    same = jnp.right_shift(ri, 6) == jnp.right_shift(rj, 6)
    incl = jnp.logical_and(same, (ri <= rj) if rev else (ri >= rj))
    strict = jnp.logical_and(same, (ri < rj) if rev else (ri > rj))
    eye = jnp.where(ri == rj, 1.0, 0.0)
    order = range(n_chunks - 1, -1, -1) if rev else range(n_chunks)

    cum_mat = jnp.concatenate([jnp.where(incl, 1.0, 0.0), jnp.where(same, 1.0, 0.0)], axis=0).astype(BF16)
    gsum = _dot_mask(cum_mat, g_all)
    gcum, gtot = gsum[:T], gsum[T:]
    gcum_t = gcum.T

    hd = []
    for hh in range(DN_HEADS):
        ls = slice(HEAD_W * hh, HEAD_W * (hh + 1))
        col = d_off + 4 + hh
        g_col = gcum[:, col:col + 1]
        g_row = gcum_t[col:col + 1, :]
        g_tot = gtot[:, col:col + 1]
        beta = beta_all[:, d_off + hh:d_off + hh + 1]
        qh = _l2norm(yq[:, ls]) * (HEAD_W ** -0.5)
        kh = _l2norm(yk[:, ls])
        decay = jnp.where(incl, jnp.exp(jnp.minimum(g_col - g_row, 0.0)), 0.0)
        eg = jnp.exp(g_col)
        kb = kh * beta
        kh16 = kh.astype(BF16)
        kq = _dot_nt(jnp.concatenate([kb.astype(BF16), qh.astype(BF16)], axis=0), kh16)
        nmat = jnp.where(strict, kq[:T] * decay, 0.0)
        hd.append(dict(
            nmat=nmat,
            rhs=jnp.concatenate([yv[:, ls] * beta, kb * eg], axis=1),
            qk=(kq[T:] * decay).astype(BF16),
            qg=(qh * eg).astype(BF16),
            kend=(kh * jnp.exp(g_tot - g_col)).astype(BF16),
            gend=jnp.exp(g_tot),
        ))

    for h in hd:
        n16 = h["nmat"].astype(BF16)
        h["xinv"] = eye - h["nmat"]
        h["pw"] = jnp.dot(n16, n16, preferred_element_type=F32).astype(BF16)
    for it in range(5):
        for h in hd:
            if it < 4:
                xp = jnp.dot(jnp.concatenate([h["xinv"].astype(BF16), h["pw"]], axis=0), h["pw"],
                             preferred_element_type=F32)
                h["xinv"] = h["xinv"] + xp[:T]
                h["pw"] = xp[T:].astype(BF16)
            else:
                h["xinv"] = h["xinv"] + jnp.dot(h["xinv"].astype(BF16), h["pw"], preferred_element_type=F32)
    for h in hd:
        h["res"] = (eye - h["xinv"]) - _dot_split(h["nmat"], h["xinv"])
    for h in hd:
        h["xinv"] = h["xinv"] + jnp.dot(h["xinv"].astype(BF16), h["res"].astype(BF16), preferred_element_type=F32)
    for h in hd:
        w = _dot_split(h["xinv"], h["rhs"])
        h["val"] = w[:, :HEAD_W]
        h["kcum"] = w[:, HEAD_W:].astype(BF16)

    ss = [s_ref[hh] for hh in range(DN_HEADS)]
    us = [[None] * n_chunks for _ in range(DN_HEADS)]
    oqs = [[None] * n_chunks for _ in range(DN_HEADS)]
    for c in order:
        rs = slice(C * c, C * (c + 1))
        for hh, h in enumerate(hd):
            ks = jnp.dot(jnp.concatenate([h["kcum"][rs], h["qg"][rs]], axis=0), ss[hh].astype(BF16),
                         preferred_element_type=F32)
            u16 = (h["val"][rs] - ks[:C]).astype(BF16)
            oqs[hh][c] = ks[C:]
            us[hh][c] = u16
            ss[hh] = ss[hh] * h["gend"][C * c:C * c + 1, :] + _dot_tn(h["kend"][rs], u16)
    o_heads = []
    for hh, h in enumerate(hd):
        s_ref[hh] = ss[hh]
        sfin_ref[0, hh] = ss[hh]
        o_heads.append(jnp.concatenate(oqs[hh], axis=0)
                       + jnp.dot(h["qk"], jnp.concatenate(us[hh], axis=0), preferred_element_type=F32))

    o = jnp.concatenate(o_heads, axis=1)
    if rev:
        o = o + of_ref[...]
        z = z_ref[...]
        cols = []
        for hh in range(DN_HEADS):
            oh = o[:, HEAD_W * hh:HEAD_W * (hh + 1)]
            cols.append(oh * lax.rsqrt(jnp.mean(oh * oh, axis=-1, keepdims=True) + EPS) * ng_ref[...])
        out_ref[...] = (jnp.concatenate(cols, axis=1) * (z * jax.nn.sigmoid(z))).astype(BF16)
    else:
        out_ref[...] = o


def _dn_dir(rev, proj, of, s0, cw, alog_row, dt_row, norm_g):
    def tile_map(t):
        return (N_SEQ_TILES - 1 - t) if rev else t

    row_blk = lambda cb_: pl.BlockSpec((SEQ_TILE, COL_BLK), lambda t: (tile_map(t), cb_))
    full = lambda shp: pl.BlockSpec(shp, lambda t: (0,) * len(shp))
    in_specs = [row_blk(CB_DN_Q), row_blk(CB_DN_K), row_blk(CB_DN_V),
                pl.BlockSpec((SEQ_TILE, HEAD_W), lambda t: (tile_map(t), COL_BA // HEAD_W))]
    args = [proj, proj, proj, proj]
    if rev:
        in_specs += [pl.BlockSpec((SEQ_TILE, BR_W), lambda t: (tile_map(t), 0)), row_blk(CB_DN_Z)]
        args += [of, proj]
    in_specs += [
        pl.BlockSpec((1, DN_HEADS, HEAD_W, HEAD_W), lambda t: (_tile_seq(tile_map(t)), 0, 0, 0)),
        full((4, 3 * BR_W)), full((1, HEAD_W)), full((1, HEAD_W)),
    ]
    args += [s0, cw, alog_row, dt_row]
    if rev:
        in_specs += [full((1, HEAD_W))]
        args += [norm_g.reshape(1, HEAD_W)]
    return pl.pallas_call(
        functools.partial(_dn_body, rev),
        grid=(N_SEQ_TILES,),
        in_specs=in_specs,
        out_specs=[
            pl.BlockSpec((SEQ_TILE, BR_W), lambda t: (tile_map(t), 0)),
            pl.BlockSpec((1, DN_HEADS, HEAD_W, HEAD_W), lambda t: (jnp.minimum(tile_map(t), CTX_TILES), 0, 0, 0)),
        ],
        out_shape=[
            jax.ShapeDtypeStruct((N_ROWS, BR_W), BF16 if rev else F32),
            jax.ShapeDtypeStruct((CTX_TILES + 1, DN_HEADS, HEAD_W, HEAD_W), F32),
        ],
        scratch_shapes=[pltpu.VMEM((3, HALO, BR_W), F32), pltpu.VMEM((DN_HEADS, HEAD_W, HEAD_W), F32)],
        compiler_params=_cparams(("arbitrary",)),
        name="deltanet_bwd" if rev else "deltanet_fwd",
    )(*args)


MG_TM = 512
MG_TN = 512


def _route_top2(logits):
    idx = lax.broadcasted_iota(jnp.int32, logits.shape, 1)
    m1 = jnp.max(logits, axis=-1, keepdims=True)
    i1 = jnp.min(jnp.where(logits == m1, idx, N_EXPERTS), axis=-1, keepdims=True)
    rest = jnp.where(idx == i1, -jnp.inf, logits)
    m2 = jnp.max(rest, axis=-1, keepdims=True)
    i2 = jnp.min(jnp.where(rest == m2, idx, N_EXPERTS), axis=-1, keepdims=True)
    e2 = jnp.exp(m2 - m1)
    w1 = 1.0 / (1.0 + e2)
    lane = lax.broadcasted_iota(jnp.int32, (logits.shape[0], 4), 1)
    return jnp.where(lane == 0, i1.astype(F32),
                     jnp.where(lane == 1, i2.astype(F32), jnp.where(lane == 2, w1, e2 * w1)))


def _merge_body(moe, *refs):
    (x_ref, mod_ref, ya_ref, yb_ref, yc_ref, yd_ref, g0_ref, g1_ref, g2_ref, g3_ref, wbr_ref, wout_ref,
     n2g_ref) = refs[:13]
    if moe:
        wr_ref, br_ref, o_ref, h_ref, route_ref, acc_ref = refs[13:]
    else:
        o_ref, h_ref, acc_ref = refs[13:]
    j = pl.program_id(1)

    @pl.when(j == 0)
    def _():
        acc_ref[...] = jnp.zeros_like(acc_ref)

    m = None
    for k, (y_ref, g_ref) in enumerate(((ya_ref, g0_ref), (yb_ref, g1_ref), (yc_ref, g2_ref), (yd_ref, g3_ref))):
        br = jnp.dot(y_ref[...], wbr_ref[k], preferred_element_type=F32)
        term = jax.nn.sigmoid(g_ref[...]) * br
        m = term if m is None else m + term
    acc_ref[...] += jnp.dot(m.astype(BF16), wout_ref[...], preferred_element_type=F32)

    @pl.when(j == pl.num_programs(1) - 1)
    def _():
        xn = x_ref[...] + mod_ref[0, 2:3, :] * acc_ref[...]
        o_ref[...] = xn
        h = _rms(xn, n2g_ref[...]) * (1.0 + mod_ref[0, 4:5, :]) + mod_ref[0, 3:4, :]
        h_ref[...] = h.astype(h_ref.dtype)
        if moe:
            route_ref[...] = _route_top2(_dot_split(h, wr_ref[...]) + br_ref[...])


def _merge(x, mod, ys, proj, w_br, w_out, norm2_g, wr=None, br=None):
    moe = wr is not None
    nj = D_MODEL // MG_TN
    y_spec = pl.BlockSpec((MG_TM, BR_W), lambda i, j: (i, 0))
    gate_spec = lambda k: pl.BlockSpec((MG_TM, MG_TN), lambda i, j: (i, CB_GATE0 + k * nj + j))
    row_spec = lambda w: pl.BlockSpec((MG_TM, w), lambda i, j: (i, 0))
    in_specs = [
        row_spec(D_MODEL),
        pl.BlockSpec((1, 6, D_MODEL), lambda i, j: (_row_group(i * MG_TM), 0, 0)),
        y_spec, y_spec, y_spec, y_spec,
        gate_spec(0), gate_spec(1), gate_spec(2), gate_spec(3),
        pl.BlockSpec((4, BR_W, MG_TN), lambda i, j: (0, 0, j)),
        pl.BlockSpec((MG_TN, D_MODEL), lambda i, j: (j, 0)),
        pl.BlockSpec((1, D_MODEL), lambda i, j: (0, 0)),
    ]
    args = [x, mod, *ys, proj, proj, proj, proj, w_br, w_out, norm2_g.reshape(1, D_MODEL)]
    out_specs = [row_spec(D_MODEL), row_spec(D_MODEL)]
    out_shape = [jax.ShapeDtypeStruct((N_ROWS, D_MODEL), F32),
                 jax.ShapeDtypeStruct((N_ROWS, D_MODEL), F32 if moe else BF16)]
    if moe:
        in_specs += [pl.BlockSpec((D_MODEL, N_EXPERTS), lambda i, j: (0, 0)),
                     pl.BlockSpec((1, N_EXPERTS), lambda i, j: (0, 0))]
        args += [wr, br.reshape(1, N_EXPERTS)]
        out_specs += [row_spec(4)]
        out_shape += [jax.ShapeDtypeStruct((N_ROWS, 4), F32)]
    return pl.pallas_call(
        functools.partial(_merge_body, moe),
        grid=(N_ROWS // MG_TM, nj),
        in_specs=in_specs,
        out_specs=out_specs,
        out_shape=out_shape,
        scratch_shapes=[pltpu.VMEM((MG_TM, D_MODEL), F32)],
        compiler_params=_cparams(("arbitrary", "arbitrary")),
        name="merge_out",
    )(*args)


FF_TM = 512
FF_TF = 512


def _ffn_body(final, h_ref, x_ref, mod_ref, fg_ref, w1_ref, w3_ref, w2_ref, o_ref, acc_ref):
    s = pl.program_id(1)

    @pl.when(s == 0)
    def _():
        acc_ref[...] = jnp.zeros_like(acc_ref)

    h = h_ref[...]
    a1 = jnp.dot(h, w1_ref[...], preferred_element_type=F32)
    a3 = jnp.dot(h, w3_ref[...], preferred_element_type=F32)
    act = a1 * jax.nn.sigmoid(a1) * a3
    acc_ref[...] += jnp.dot(act.astype(BF16), w2_ref[...], preferred_element_type=F32)

    @pl.when(s == pl.num_programs(1) - 1)
    def _():
        xn = x_ref[...] + mod_ref[0, 5:6, :] * acc_ref[...]
        o_ref[...] = _rms(xn, fg_ref[...]) if final else xn


def _ffn(h, x, mod, final_g, w1, w3, w2, final):
    ff = w1.shape[1]
    return pl.pallas_call(
        functools.partial(_ffn_body, final),
        grid=(N_ROWS // FF_TM, ff // FF_TF),
        in_specs=[
            pl.BlockSpec((FF_TM, D_MODEL), lambda i, s: (i, 0)),
            pl.BlockSpec((FF_TM, D_MODEL), lambda i, s: (i, 0)),
            pl.BlockSpec((1, 6, D_MODEL), lambda i, s: (_row_group(i * FF_TM), 0, 0)),
            pl.BlockSpec((1, D_MODEL), lambda i, s: (0, 0)),
            pl.BlockSpec((D_MODEL, FF_TF), lambda i, s: (0, s)),
            pl.BlockSpec((D_MODEL, FF_TF), lambda i, s: (0, s)),
            pl.BlockSpec((FF_TF, D_MODEL), lambda i, s: (s, 0)),
        ],
        out_specs=pl.BlockSpec((FF_TM, D_MODEL), lambda i, s: (i, 0)),
        out_shape=jax.ShapeDtypeStruct((N_ROWS, D_MODEL), F32),
        scratch_shapes=[pltpu.VMEM((FF_TM, D_MODEL), F32)],
        compiler_params=_cparams(("arbitrary", "arbitrary")),
        name="dense_ffn",
    )(h, x, mod, final_g.reshape(1, D_MODEL), w1, w3, w2)


TOP_K = 2
N_ASSIGN = TOP_K * N_ROWS
MOE_TM = 512
MOE_TF = 256
MOE_TILES = 72
MOE_ROWS = MOE_TILES * MOE_TM
CB_TM = 256


def _route_plan(route):
    ids = route[:, :TOP_K].astype(jnp.int32).reshape(-1)
    onehot = (ids[:, None] == jnp.arange(N_EXPERTS, dtype=jnp.int32)[None, :]).astype(jnp.int32)
    csum = jnp.cumsum(onehot, axis=0)
    counts = csum[-1]
    padded = ((counts + MOE_TM - 1) // MOE_TM) * MOE_TM
    end_pad = jnp.cumsum(padded)
    start_pad = end_pad - padded
    pos = jnp.sum(onehot * (csum - 1 + start_pad[None, :]), axis=1).astype(jnp.int32)
    token = jnp.arange(N_ASSIGN, dtype=jnp.int32) // TOP_K
    n_used = (end_pad[-1] // MOE_TM).astype(jnp.int32).reshape(1)
    tile = jnp.arange(MOE_TILES, dtype=jnp.int32)
    tile_expert = jnp.minimum(jnp.sum((tile[:, None] * MOE_TM >= end_pad[None, :]).astype(jnp.int32), axis=1),
                              N_EXPERTS - 1).astype(jnp.int32)
    k_in_expert = tile - start_pad[tile_expert] // MOE_TM
    cnt = jnp.where(tile < n_used[0], jnp.clip(counts[tile_expert] - k_in_expert * MOE_TM, 0, MOE_TM), 0)
    valid = jnp.arange(MOE_TM, dtype=jnp.int32)[None, :] < cnt[:, None]
    rows = jnp.arange(MOE_ROWS, dtype=jnp.int32).reshape(MOE_TILES, MOE_TM)
    free_rows = lax.sort(jnp.where(valid, MOE_ROWS, rows).reshape(-1))[:MOE_ROWS - N_ASSIGN]
    _, row_token = lax.sort((jnp.concatenate([pos, free_rows]),
                             jnp.concatenate([token, jnp.zeros((MOE_ROWS - N_ASSIGN,), jnp.int32)])), num_keys=1)
    return pos, row_token, tile_expert, n_used


def _start_row_gather(idx_ref, n_rows, src_ref, dst_ref, sem):
    def issue(p, c):
        for u in range(4):
            r = 4 * p + u
            pltpu.make_async_copy(src_ref.at[pl.ds(idx_ref[0, 0, r], 1)], dst_ref.at[pl.ds(r, 1)], sem).start(priority=u % 2)
        return c

    lax.fori_loop(0, n_rows // 4, issue, 0)


def _wait_row_gather(n_rows, src_ref, dst_ref, sem):
    pltpu.make_async_copy(src_ref.at[pl.ds(0, n_rows)], dst_ref, sem).wait()


def _moe_ffn_body(te_ref, nu_ref, tok_ref, tok_next_ref, src_ref, w1_ref, w3_ref, w2_ref, o_ref,
                  gbuf_ref, h_ref, sems):
    i = pl.program_id(0)
    f = pl.program_id(1)
    n_used = nu_ref[0]
    used = i < n_used
    slot = lax.rem(i, 2)

    def start_tile(idx_ref, s):
        _start_row_gather(idx_ref, MOE_TM, src_ref, gbuf_ref.at[s], sems.at[s])

    @pl.when(jnp.logical_and(f == 0, i == 0))
    def _():
        start_tile(tok_ref, 0)

    @pl.when(f == 0)
    def _():
        o_ref[...] = jnp.zeros_like(o_ref)

        @pl.when(used)
        def _():
            _wait_row_gather(MOE_TM, src_ref, gbuf_ref.at[slot], sems.at[slot])
            h_ref[...] = gbuf_ref[slot].astype(BF16)

        @pl.when(i + 1 < n_used)
        def _():
            start_tile(tok_next_ref, 1 - slot)

    @pl.when(used)
    def _():
        h = h_ref[...]
        a1 = jnp.dot(h, w1_ref[0], preferred_element_type=F32)
        a3 = jnp.dot(h, w3_ref[0], preferred_element_type=F32)
        act = a1 * jax.nn.sigmoid(a1) * a3
        o_ref[...] += jnp.dot(act.astype(BF16), w2_ref[0], preferred_element_type=F32)


def _moe_ffn(tile_expert, n_used, row_token, src, w1, w3, w2):
    nf = w1.shape[2] // MOE_TF

    def f_blk(i, f, nu):
        return jnp.where(i < nu[0], f, nf - 1)

    tok = row_token.reshape(MOE_TILES, 1, MOE_TM)
    return pl.pallas_call(
        _moe_ffn_body,
        grid_spec=pltpu.PrefetchScalarGridSpec(
            num_scalar_prefetch=2,
            grid=(MOE_TILES, nf),
            in_specs=[
                pl.BlockSpec((1, 1, MOE_TM), lambda i, f, te, nu: (i, 0, 0), memory_space=pltpu.SMEM),
                pl.BlockSpec((1, 1, MOE_TM), lambda i, f, te, nu: (jnp.minimum(i + 1, MOE_TILES - 1), 0, 0),
                             memory_space=pltpu.SMEM),
                pl.BlockSpec(memory_space=pl.ANY),
                pl.BlockSpec((1, D_MODEL, MOE_TF), lambda i, f, te, nu: (te[i], 0, f_blk(i, f, nu))),
                pl.BlockSpec((1, D_MODEL, MOE_TF), lambda i, f, te, nu: (te[i], 0, f_blk(i, f, nu))),
                pl.BlockSpec((1, MOE_TF, D_MODEL), lambda i, f, te, nu: (te[i], f_blk(i, f, nu), 0)),
            ],
            out_specs=pl.BlockSpec((MOE_TM, D_MODEL), lambda i, f, te, nu: (i, 0)),
            scratch_shapes=[pltpu.VMEM((2, MOE_TM, D_MODEL), F32), pltpu.VMEM((MOE_TM, D_MODEL), BF16),
                            pltpu.SemaphoreType.DMA((2,))],
        ),
        out_shape=jax.ShapeDtypeStruct((MOE_ROWS, D_MODEL), F32),
        compiler_params=_cparams(("arbitrary", "arbitrary")),
        name="moe_ffn",
    )(tile_expert, n_used, tok, tok, src, w1, w3, w2)


CB_CTX_TILES = N_CTX_ROWS // CB_TM


def _combine_body(final, pos_ref, pos_next_ref, ys_ref, route_ref, x_ref, mod_ref, fg_ref, *refs):
    out_refs, (buf_ref, sems) = refs[:-2], refs[-2:]
    i = pl.program_id(0)
    slot = lax.rem(i, 2)
    n_rows = TOP_K * CB_TM

    @pl.when(i == 0)
    def _():
        _start_row_gather(pos_ref, n_rows, ys_ref, buf_ref.at[0], sems.at[0])

    @pl.when(i + 1 < pl.num_programs(0))
    def _():
        _start_row_gather(pos_next_ref, n_rows, ys_ref, buf_ref.at[1 - slot], sems.at[1 - slot])

    _wait_row_gather(n_rows, ys_ref, buf_ref.at[slot], sems.at[slot])
    route = route_ref[...]
    y = route[:, 2:3] * buf_ref[slot, 0:CB_TM, :] + route[:, 3:4] * buf_ref[slot, CB_TM:, :]
    xn = x_ref[...] + mod_ref[0, 5:6, :] * y
    if final:
        res = _rms(xn, fg_ref[...])

        @pl.when(i < CB_CTX_TILES)
        def _():
            out_refs[0][...] = res

        @pl.when(i >= CB_CTX_TILES)
        def _():
            out_refs[1][...] = res
    else:
        out_refs[0][...] = xn


def _combine(pos, ys, route, x, mod, final_g, final):
    n_tiles = N_ROWS // CB_TM
    pos_t = pos.reshape(n_tiles, CB_TM, TOP_K).transpose(0, 2, 1).reshape(n_tiles, 1, TOP_K * CB_TM)
    if final:
        out_specs = [pl.BlockSpec((CB_TM, D_MODEL), lambda i: (jnp.minimum(i, CB_CTX_TILES - 1), 0)),
                     pl.BlockSpec((CB_TM, D_MODEL), lambda i: (jnp.maximum(i - CB_CTX_TILES, 0), 0))]
        out_shape = [jax.ShapeDtypeStruct((N_CTX_ROWS, D_MODEL), F32),
                     jax.ShapeDtypeStruct((N_ROWS - N_CTX_ROWS, D_MODEL), F32)]
    else:
        out_specs = [pl.BlockSpec((CB_TM, D_MODEL), lambda i: (i, 0))]
        out_shape = [jax.ShapeDtypeStruct((N_ROWS, D_MODEL), F32)]
    return pl.pallas_call(
        functools.partial(_combine_body, final),
        grid=(n_tiles,),
        in_specs=[
            pl.BlockSpec((1, 1, TOP_K * CB_TM), lambda i: (i, 0, 0), memory_space=pltpu.SMEM),
            pl.BlockSpec((1, 1, TOP_K * CB_TM), lambda i: (jnp.minimum(i + 1, n_tiles - 1), 0, 0),
                         memory_space=pltpu.SMEM),
            pl.BlockSpec(memory_space=pl.ANY),
            pl.BlockSpec((CB_TM, 4), lambda i: (i, 0)),
            pl.BlockSpec((CB_TM, D_MODEL), lambda i: (i, 0)),
            pl.BlockSpec((1, 6, D_MODEL), lambda i: (_row_group(i * CB_TM), 0, 0)),
            pl.BlockSpec((1, D_MODEL), lambda i: (0, 0)),
        ],
        out_specs=out_specs,
        out_shape=out_shape,
        scratch_shapes=[pltpu.VMEM((2, TOP_K * CB_TM, D_MODEL), F32), pltpu.SemaphoreType.DMA((2,))],
        compiler_params=_cparams(("arbitrary",)),
        name="moe_combine",
    )(pos_t, pos_t, ys, route, x, mod, final_g.reshape(1, D_MODEL))


def _ba_row(v):
    row = jnp.zeros((1, HEAD_W), F32)
    for d in range(2):
        row = row.at[0, d * 8 + 4:d * 8 + 8].set(v[d])
    return row


def kernel(x_prompt, x_sample, state_rglru, state_delta, c, c_ctx, ada_w, ada_b, norm1_g, norm2_g, w_in, rg_conv_w, rg_conv_b, rg_wa, rg_ba, rg_wx, rg_bx, rg_lam, pool_w, pool_scale, sgu_ln_g, sgu_ln_b, sgu_ws, sgu_bs, dn_conv_w, dn_a_log, dn_dt_bias, dn_norm_g, w_br, w_out, ffn_w1, ffn_w3, ffn_w2, moe_wr, moe_br, moe_w1, moe_w3, moe_w2, final_g):
    x = jnp.concatenate([x_prompt.reshape(N_CTX_ROWS, D_MODEL), x_sample.reshape(N_ROWS - N_CTX_ROWS, D_MODEL)], axis=0)
    c_all = jnp.concatenate([c_ctx[None, :], c, jnp.zeros((8 - 1 - N_LAT_SEQ, D_MODEL), F32)], axis=0)
    mods = _ada(c_all, ada_w, ada_b).reshape(DEPTH, 8, 6, D_MODEL)

    rg_states, dn_states = [], []
    for l in range(DEPTH):
        mod = mods[l]
        w_mix = jnp.pad(w_in[l][:, :COL_BA + 16].astype(BF16), ((0, 0), (0, COL_BLK - 16)))
        w_gate = w_in[l][:, COL_BA + 16:].astype(BF16)
        proj = _inproj(x, mod, norm1_g[l], w_mix, w_gate)

        rg_h0 = [jnp.concatenate([jnp.zeros((N_CTX_SEQ, BR_W), F32), state_rglru[:, l, d]], axis=0)[:, None, :]
                 for d in range(2)]
        rg_args = lambda d: (rg_conv_w[l, d], rg_conv_b[l, d], rg_wa[l, d], rg_ba[l, d], rg_wx[l, d],
                             rg_bx[l, d], rg_lam[l, d])
        hf, st_f = _rglru_dir(False, proj, None, rg_h0[0], *rg_args(0))
        y_a, st_b = _rglru_dir(True, proj, hf, rg_h0[1], *rg_args(1))
        rg_states.append(jnp.stack([st_f[:CTX_TILES, 0], st_b[:CTX_TILES, 0]], axis=1))

        y_b = _pool(proj, pool_w[l], pool_scale[l])
        y_c = _sgu(proj, sgu_ln_g[l], sgu_ln_b[l], sgu_ws[l], sgu_bs[l])

        dn_s0 = [jnp.concatenate([jnp.zeros((N_CTX_SEQ, DN_HEADS, HEAD_W, HEAD_W), F32), state_delta[:, l, d]], axis=0)
                 for d in range(2)]
        alog_row, dt_row = _ba_row(dn_a_log[l]), _ba_row(dn_dt_bias[l])
        of, s_f = _dn_dir(False, proj, None, dn_s0[0], dn_conv_w[l, 0], alog_row, dt_row, None)
        y_d, s_b = _dn_dir(True, proj, of, dn_s0[1], dn_conv_w[l, 1], alog_row, dt_row, dn_norm_g[l])
        dn_states.append(jnp.stack([s_f[:CTX_TILES], s_b[:CTX_TILES]], axis=1))

        j = l // 2
        final = l == DEPTH - 1
        merge_args = (x, mod, (y_a, y_b, y_c, y_d), proj, w_br[l].astype(BF16), w_out[l].astype(BF16), norm2_g[l])
        if l % 2 == 0:
            x, h2 = _merge(*merge_args)
            x = _ffn(h2, x, mod, final_g, ffn_w1[j].astype(BF16), ffn_w3[j].astype(BF16),
                     ffn_w2[j].astype(BF16), final)
        else:
            x, h2, route = _merge(*merge_args, moe_wr[j], moe_br[j])
            pos, row_token, tile_expert, n_used = _route_plan(route)
            ys = _moe_ffn(tile_expert, n_used, row_token, h2, moe_w1[j].astype(BF16), moe_w3[j].astype(BF16),
                          moe_w2[j].astype(BF16))
            outs = _combine(pos, ys, route, x, mod, final_g, final)
            x = outs[0] if not final else None

    if x is None:
        y_prompt, y_sample = outs
    else:
        y_prompt, y_sample = x[:N_CTX_ROWS], x[N_CTX_ROWS:]
    return (y_prompt.reshape(N_CTX_SEQ, CTX_LEN, D_MODEL), y_sample.reshape(N_LAT_SEQ, LAT_LEN, D_MODEL),
            jnp.stack(rg_states, axis=1), jnp.stack(dn_states, axis=1))
```

```python
import functools

import jax
import jax.numpy as jnp
from jax import lax
from jax.experimental import pallas as pl
from jax.experimental.pallas import tpu as pltpu

F32 = jnp.float32
BF16 = jnp.bfloat16

D_MODEL = 2048
DEPTH = 2
N_CTX_SEQ = 32
CTX_LEN = 256
N_LAT_SEQ = 2
LAT_LEN = 4096
GRID_W = 64
N_CTX_ROWS = N_CTX_SEQ * CTX_LEN
N_ROWS = N_CTX_ROWS + N_LAT_SEQ * LAT_LEN
BR_W = 512
RG_C = 8.0
POOL_WINDOWS = (2, 4, 8, 16)
SGU_CHUNK = 128
DN_CHUNK = 64
DN_HEADS = 4
HEAD_W = 128
N_EXPERTS = 8
EPS = 1e-6

SEQ_TILE = 256
N_SEQ_TILES = N_ROWS // SEQ_TILE
CTX_TILES = N_CTX_ROWS // SEQ_TILE
LAT_TILES_PER_SEQ = LAT_LEN // SEQ_TILE
HALO = 8

COL_BLK = 512
CB_RG_X, CB_RG_G, CB_POOL, CB_SGU_U, CB_SGU_V, CB_DN_Q, CB_DN_K, CB_DN_V, CB_DN_Z = range(9)
COL_BA = 4608
CB_GATE0 = 10
P_COLS = 5120 + 4 * D_MODEL

VMEM_LIMIT = 56 * 1024 * 1024


def _cparams(sem, vmem_limit=VMEM_LIMIT):
    return pltpu.CompilerParams(dimension_semantics=sem, vmem_limit_bytes=vmem_limit)


def _row_group(row0):
    return jnp.where(row0 < N_CTX_ROWS, 0, 1 + (row0 - N_CTX_ROWS) // LAT_LEN)


def _tile_seq(tt):
    return jnp.where(tt < CTX_TILES, tt, CTX_TILES + (tt - CTX_TILES) // LAT_TILES_PER_SEQ)


def _neg_expm1_2x(log_a, a):
    x = 2.0 * log_a
    series = -x * (1.0 + x * (0.5 + x * (1.0 / 6.0)))
    return jnp.where(jnp.abs(x) < 0.01, series, 1.0 - a * a)


def _rms(x, g):
    return x * lax.rsqrt(jnp.mean(x * x, axis=-1, keepdims=True) + EPS) * g


def _split16(a):
    ah = a.astype(BF16)
    return ah, (a - ah.astype(F32)).astype(BF16)


def _dot_mask(m16, x):
    n = x.shape[1]
    xh = x.astype(BF16)
    r = x - xh.astype(F32)
    xm = r.astype(BF16)
    xl = (r - xm.astype(F32)).astype(BF16)
    y = jnp.dot(m16, jnp.concatenate([xh, xm, xl], axis=1), preferred_element_type=F32)
    return y[:, :n] + y[:, n:2 * n] + y[:, 2 * n:]


def _ada_body(c_ref, w_ref, b_ref, o_ref):
    c = c_ref[...]
    s = (c * jax.nn.sigmoid(c)).astype(BF16)
    o_ref[0] = jnp.dot(s, w_ref[0].astype(BF16), preferred_element_type=F32) + b_ref[0]


def _ada(c_all, ada_w, ada_b):
    tn = 512
    nj = 6 * D_MODEL // tn
    return pl.pallas_call(
        _ada_body,
        grid=(DEPTH, nj),
        in_specs=[
            pl.BlockSpec((8, D_MODEL), lambda l, j: (0, 0)),
            pl.BlockSpec((1, D_MODEL, tn), lambda l, j: (l, 0, j)),
            pl.BlockSpec((1, 1, tn), lambda l, j: (l, 0, j)),
        ],
        out_specs=pl.BlockSpec((1, 8, tn), lambda l, j: (l, 0, j)),
        out_shape=jax.ShapeDtypeStruct((DEPTH, 8, 6 * D_MODEL), F32),
        compiler_params=_cparams(("arbitrary", "arbitrary")),
        name="ada_mod",
    )(c_all, ada_w, ada_b.reshape(DEPTH, 1, 6 * D_MODEL))


IN_TM = 1024
IN_TN = 1024


IN_NA = (CB_GATE0 * COL_BLK) // IN_TN
IN_VMEM_LIMIT = 60 * 1024 * 1024


def _inproj_body(x_ref, mod_ref, g_ref, wa_ref, wg_ref, om_ref, og_ref, h_ref):
    j = pl.program_id(1)

    @pl.when(j == 0)
    def _():
        y = _rms(x_ref[...], g_ref[...])
        h = y * (1.0 + mod_ref[0, 1:2, :]) + mod_ref[0, 0:1, :]
        h_ref[...] = h.astype(BF16)

    @pl.when(j < IN_NA)
    def _():
        om_ref[...] = jnp.dot(h_ref[...], wa_ref[...], preferred_element_type=F32)

    @pl.when(j >= IN_NA)
    def _():
        og_ref[...] = jnp.dot(h_ref[...], wg_ref[...], preferred_element_type=F32).astype(BF16)


def _inproj(x, mod, norm_g, w_mix, w_gate):
    mix_cols, gate_cols = w_mix.shape[1], w_gate.shape[1]
    return pl.pallas_call(
        _inproj_body,
        grid=(N_ROWS // IN_TM, (mix_cols + gate_cols) // IN_TN),
        in_specs=[
            pl.BlockSpec((IN_TM, D_MODEL), lambda i, j: (i, 0)),
            pl.BlockSpec((1, 6, D_MODEL), lambda i, j: (_row_group(i * IN_TM), 0, 0)),
            pl.BlockSpec((1, D_MODEL), lambda i, j: (0, 0)),
            pl.BlockSpec((D_MODEL, IN_TN), lambda i, j: (0, jnp.minimum(j, IN_NA - 1))),
            pl.BlockSpec((D_MODEL, IN_TN), lambda i, j: (0, jnp.maximum(j - IN_NA, 0))),
        ],
        out_specs=[
            pl.BlockSpec((IN_TM, IN_TN), lambda i, j: (i, jnp.minimum(j, IN_NA - 1))),
            pl.BlockSpec((IN_TM, IN_TN), lambda i, j: (i, jnp.maximum(j - IN_NA, 0))),
        ],
        out_shape=[jax.ShapeDtypeStruct((N_ROWS, mix_cols), F32), jax.ShapeDtypeStruct((N_ROWS, gate_cols), BF16)],
        scratch_shapes=[pltpu.VMEM((IN_TM, D_MODEL), BF16)],
        compiler_params=_cparams(("arbitrary", "arbitrary"), IN_VMEM_LIMIT),
        name="in_proj",
    )(x, mod, norm_g.reshape(1, D_MODEL), w_mix, w_gate)


def _scan_tile(rev, t):
    tt = (N_SEQ_TILES - 1 - t) if rev else t
    edge = (LAT_TILES_PER_SEQ - 1) if rev else 0
    first = jnp.logical_or(tt < CTX_TILES, ((tt - CTX_TILES) % LAT_TILES_PER_SEQ) == edge)
    return tt, first


def _conv_taps(rev, x, halo):
    n = x.shape[0]
    if not rev:
        ext = jnp.concatenate([halo, x], axis=0)
        return lambda k: x if k == 0 else pltpu.roll(ext, k, axis=0)[HALO:]
    ext = jnp.concatenate([x, halo], axis=0)
    return lambda k: x if k == 0 else pltpu.roll(ext, n + HALO - k, axis=0)[:n]


def _short_conv(rev, x, halo, cw):
    tap = _conv_taps(rev, x, halo)
    acc = tap(3) * cw[0:1, :]
    for j in range(1, 4):
        acc = acc + tap(3 - j) * cw[j:j + 1, :]
    return acc


def _next_halo(rev, x):
    return x[0:HALO] if rev else x[x.shape[0] - HALO:]


def _rglru_body(rev, *refs):
    if rev:
        (xr_ref, hf_ref, xg_ref, h0_ref, cw_ref, cb_ref, wa_ref, ba_ref, wx_ref, bx_ref, lam_ref,
         out_ref, st_ref, halo_ref, carry_ref) = refs
    else:
        (xr_ref, h0_ref, cw_ref, cb_ref, wa_ref, ba_ref, wx_ref, bx_ref, lam_ref,
         out_ref, st_ref, halo_ref, carry_ref) = refs
    _, first = _scan_tile(rev, pl.program_id(0))

    @pl.when(first)
    def _():
        halo_ref[...] = jnp.zeros_like(halo_ref)
        carry_ref[...] = h0_ref[0]

    x = xr_ref[...]
    xc = _short_conv(rev, x, halo_ref[...], cw_ref[...]) + cb_ref[...]
    halo_ref[...] = _next_halo(rev, x)

    xcb = xc.astype(BF16)
    rs, gs = [], []
    for hh in range(4):
        xh = xcb[:, HEAD_W * hh:HEAD_W * (hh + 1)]
        rs.append(jnp.dot(xh, wa_ref[hh], preferred_element_type=F32))
        gs.append(jnp.dot(xh, wx_ref[hh], preferred_element_type=F32))
    r = jax.nn.sigmoid(jnp.concatenate(rs, axis=1) + ba_ref[...])
    gi = jax.nn.sigmoid(jnp.concatenate(gs, axis=1) + bx_ref[...])
    log_a = (-RG_C * jax.nn.softplus(-lam_ref[...])) * r
    a = jnp.exp(log_a)
    b = jnp.sqrt(_neg_expm1_2x(log_a, a)) * gi * xc

    n = SEQ_TILE
    row = lax.broadcasted_iota(jnp.int32, (n, BR_W), 0)
    k = 1
    while k < n:
        if rev:
            keep = row < n - k
            a_s = jnp.where(keep, pltpu.roll(a, n - k, axis=0), 1.0)
            b_s = jnp.where(keep, pltpu.roll(b, n - k, axis=0), 0.0)
        else:
            keep = row >= k
            a_s = jnp.where(keep, pltpu.roll(a, k, axis=0), 1.0)
            b_s = jnp.where(keep, pltpu.roll(b, k, axis=0), 0.0)
        b = a * b_s + b
        a = a * a_s
        k *= 2
    h = b + a * carry_ref[...]
    last = h[0:1] if rev else h[n - 1:n]
    carry_ref[...] = last
    st_ref[0] = last
    if rev:
        out_ref[...] = (jax.nn.gelu(xg_ref[...]) * (hf_ref[...] + h)).astype(BF16)
    else:
        out_ref[...] = h


def _rglru_dir(rev, proj, hf, h0, cw, cb, wa, ba, wx, bx, lam):
    def tile_map(t):
        return (N_SEQ_TILES - 1 - t) if rev else t

    row_blk = lambda cb_: pl.BlockSpec((SEQ_TILE, COL_BLK), lambda t: (tile_map(t), cb_))
    full = lambda shp: pl.BlockSpec(shp, lambda t: (0,) * len(shp))
    in_specs = [row_blk(CB_RG_X)]
    args = [proj]
    if rev:
        in_specs += [pl.BlockSpec((SEQ_TILE, BR_W), lambda t: (tile_map(t), 0)), row_blk(CB_RG_G)]
        args += [hf, proj]
    in_specs += [
        pl.BlockSpec((1, 1, BR_W), lambda t: (_tile_seq(tile_map(t)), 0, 0)),
        full((4, BR_W)), full((1, BR_W)), full((4, HEAD_W, HEAD_W)), full((1, BR_W)),
        full((4, HEAD_W, HEAD_W)), full((1, BR_W)), full((1, BR_W)),
    ]
    args += [h0, cw, cb.reshape(1, BR_W), wa.astype(BF16), ba.reshape(1, BR_W), wx.astype(BF16),
             bx.reshape(1, BR_W), lam.reshape(1, BR_W)]
    return pl.pallas_call(
        functools.partial(_rglru_body, rev),
        grid=(N_SEQ_TILES,),
        in_specs=in_specs,
        out_specs=[
            pl.BlockSpec((SEQ_TILE, BR_W), lambda t: (tile_map(t), 0)),
            pl.BlockSpec((1, 1, BR_W), lambda t: (tile_map(t), 0, 0)),
        ],
        out_shape=[
            jax.ShapeDtypeStruct((N_ROWS, BR_W), BF16 if rev else F32),
            jax.ShapeDtypeStruct((N_SEQ_TILES, 1, BR_W), F32),
        ],
        scratch_shapes=[pltpu.VMEM((HALO, BR_W), F32), pltpu.VMEM((1, BR_W), F32)],
        compiler_params=_cparams(("arbitrary",)),
        name="rglru_bwd" if rev else "rglru_fwd",
    )(*args)


def _pool_body(x_ref, w_ref, sc_ref, o_ref):
    shift = jnp.where(pl.program_id(0) < CTX_TILES, 8, 6)
    t = lax.broadcasted_iota(jnp.int32, (SEQ_TILE, SEQ_TILE), 0)
    s = lax.broadcasted_iota(jnp.int32, (SEQ_TILE, SEQ_TILE), 1)
    same = jnp.right_shift(t, shift) == jnp.right_shift(s, shift)
    d = s - t
    x = x_ref[...]
    outs = []
    for j, w in enumerate(POOL_WINDOWS):
        inwin = jnp.logical_and(jnp.logical_and(d >= -(w // 2), d <= w - 1 - w // 2), same)
        m = jnp.where(inwin, 1.0, 0.0)
        cnt = jnp.sum(m, axis=1, keepdims=True)
        xg = x[:, HEAD_W * j:HEAD_W * (j + 1)]
        ssum = _dot_mask(m.astype(BF16), xg)
        pooled = ssum / cnt - xg
        outs.append(jnp.dot(pooled.astype(BF16), w_ref[j], preferred_element_type=F32))
    o_ref[...] = (jnp.concatenate(outs, axis=1) * sc_ref[...]).astype(BF16)


def _pool(proj, w_pool, scale):
    return pl.pallas_call(
        _pool_body,
        grid=(N_SEQ_TILES,),
        in_specs=[
            pl.BlockSpec((SEQ_TILE, COL_BLK), lambda t: (t, CB_POOL)),
            pl.BlockSpec((4, HEAD_W, HEAD_W), lambda t: (0, 0, 0)),
            pl.BlockSpec((1, BR_W), lambda t: (0, 0)),
        ],
        out_specs=pl.BlockSpec((SEQ_TILE, BR_W), lambda t: (t, 0)),
        out_shape=jax.ShapeDtypeStruct((N_ROWS, BR_W), BF16),
        compiler_params=_cparams(("arbitrary",)),
        name="pool_mixer",
    )(proj, w_pool.astype(BF16), scale.reshape(1, BR_W))


def _sgu_body(u_ref, v_ref, lg_ref, lb_ref, ws_ref, bst_ref, o_ref):
    u = jax.nn.gelu(u_ref[...])
    v = jax.nn.gelu(v_ref[...])
    mu = jnp.mean(v, axis=-1, keepdims=True)
    var = jnp.mean(jnp.square(v - mu), axis=-1, keepdims=True)
    vb = ((v - mu) * lax.rsqrt(var + EPS) * lg_ref[...] + lb_ref[...]).astype(BF16)
    rows = []
    for cc in range(SEQ_TILE // SGU_CHUNK):
        cols = []
        for hh in range(4):
            blk = vb[SGU_CHUNK * cc:SGU_CHUNK * (cc + 1), HEAD_W * hh:HEAD_W * (hh + 1)]
            cols.append(jnp.dot(ws_ref[hh], blk, preferred_element_type=F32) + bst_ref[:, hh:hh + 1])
        rows.append(jnp.concatenate(cols, axis=1))
    o_ref[...] = (u * jnp.concatenate(rows, axis=0)).astype(BF16)


def _sgu(proj, ln_g, ln_b, ws, bs):
    return pl.pallas_call(
        _sgu_body,
        grid=(N_SEQ_TILES,),
        in_specs=[
            pl.BlockSpec((SEQ_TILE, COL_BLK), lambda t: (t, CB_SGU_U)),
            pl.BlockSpec((SEQ_TILE, COL_BLK), lambda t: (t, CB_SGU_V)),
            pl.BlockSpec((1, BR_W), lambda t: (0, 0)),
            pl.BlockSpec((1, BR_W), lambda t: (0, 0)),
            pl.BlockSpec((4, SGU_CHUNK, SGU_CHUNK), lambda t: (0, 0, 0)),
            pl.BlockSpec((SGU_CHUNK, 4), lambda t: (0, 0)),
        ],
        out_specs=pl.BlockSpec((SEQ_TILE, BR_W), lambda t: (t, 0)),
        out_shape=jax.ShapeDtypeStruct((N_ROWS, BR_W), BF16),
        compiler_params=_cparams(("arbitrary",)),
        name="sgu_mixer",
    )(proj, proj, ln_g.reshape(1, BR_W), ln_b.reshape(1, BR_W), ws.astype(BF16), bs.T)


def _dot_nt(a, b):
    return lax.dot_general(a, b, (((1,), (1,)), ((), ())), preferred_element_type=F32)


def _dot_tn(a, b):
    return lax.dot_general(a, b, (((0,), (0,)), ((), ())), preferred_element_type=F32)


def _dot_split(a, b):
    m = a.shape[0]
    ah, al = _split16(a)
    bh, bl = _split16(b)
    top = jnp.dot(jnp.concatenate([ah, al], axis=0), bh, preferred_element_type=F32)
    return top[:m] + top[m:] + jnp.dot(ah, bl, preferred_element_type=F32)


def _l2norm(x):
    return x * lax.rsqrt(jnp.sum(x * x, axis=-1, keepdims=True) + EPS)


def _dn_body(rev, *refs):
    if rev:
        (q_ref, k_ref, v_ref, ba_ref, of_ref, z_ref, s0_ref, cw_ref, alog_ref, dt_ref, ng_ref,
         out_ref, sfin_ref, halo_ref, s_ref) = refs
    else:
        (q_ref, k_ref, v_ref, ba_ref, s0_ref, cw_ref, alog_ref, dt_ref,
         out_ref, sfin_ref, halo_ref, s_ref) = refs
    _, first = _scan_tile(rev, pl.program_id(0))

    @pl.when(first)
    def _():
        halo_ref[...] = jnp.zeros_like(halo_ref)
        s_ref[...] = s0_ref[0]

    ys = []
    for p, ref in enumerate((q_ref, k_ref, v_ref)):
        x = ref[...]
        y = _short_conv(rev, x, halo_ref[p], cw_ref[:, BR_W * p:BR_W * (p + 1)])
        halo_ref[p] = _next_halo(rev, x)
        ys.append(y * jax.nn.sigmoid(y))
    yq, yk, yv = ys

    ba = ba_ref[...]
    beta_all = jax.nn.sigmoid(ba)
    g_all = -jnp.exp(alog_ref[...]) * jax.nn.softplus(ba + dt_ref[...])
    d_off = 8 if rev else 0

    T, C = SEQ_TILE, DN_CHUNK
    n_chunks = T // C
    ri = lax.broadcasted_iota(jnp.int32, (T, T), 0)
    rj = lax.broadcasted_iota(jnp.int32, (T, T), 1)
    same = jnp.right_shift(ri, 6) == jnp.right_shift(rj, 6)
    incl = jnp.logical_and(same, (ri <= rj) if rev else (ri >= rj))
    strict = jnp.logical_and(same, (ri < rj) if rev else (ri > rj))
    eye = jnp.where(ri == rj, 1.0, 0.0)
    order = range(n_chunks - 1, -1, -1) if rev else range(n_chunks)

    cum_mat = jnp.concatenate([jnp.where(incl, 1.0, 0.0), jnp.where(same, 1.0, 0.0)], axis=0).astype(BF16)
    gsum = _dot_mask(cum_mat, g_all)
    gcum, gtot = gsum[:T], gsum[T:]
    gcum_t = gcum.T

    hd = []
    for hh in range(DN_HEADS):
        ls = slice(HEAD_W * hh, HEAD_W * (hh + 1))
        col = d_off + 4 + hh
        g_col = gcum[:, col:col + 1]
        g_row = gcum_t[col:col + 1, :]
        g_tot = gtot[:, col:col + 1]
        beta = beta_all[:, d_off + hh:d_off + hh + 1]
        qh = _l2norm(yq[:, ls]) * (HEAD_W ** -0.5)
        kh = _l2norm(yk[:, ls])
        decay = jnp.where(incl, jnp.exp(jnp.minimum(g_col - g_row, 0.0)), 0.0)
        eg = jnp.exp(g_col)
        kb = kh * beta
        kh16 = kh.astype(BF16)
        kq = _dot_nt(jnp.concatenate([kb.astype(BF16), qh.astype(BF16)], axis=0), kh16)
        nmat = jnp.where(strict, kq[:T] * decay, 0.0)
        hd.append(dict(
            nmat=nmat,
            rhs=jnp.concatenate([yv[:, ls] * beta, kb * eg], axis=1),
            qk=(kq[T:] * decay).astype(BF16),
            qg=(qh * eg).astype(BF16),
            kend=(kh * jnp.exp(g_tot - g_col)).astype(BF16),
            gend=jnp.exp(g_tot),
        ))

    for h in hd:
        n16 = h["nmat"].astype(BF16)
        h["xinv"] = eye - h["nmat"]
        h["pw"] = jnp.dot(n16, n16, preferred_element_type=F32).astype(BF16)
    for it in range(5):
        for h in hd:
            if it < 4:
                xp = jnp.dot(jnp.concatenate([h["xinv"].astype(BF16), h["pw"]], axis=0), h["pw"],
                             preferred_element_type=F32)
                h["xinv"] = h["xinv"] + xp[:T]
                h["pw"] = xp[T:].astype(BF16)
            else:
                h["xinv"] = h["xinv"] + jnp.dot(h["xinv"].astype(BF16), h["pw"], preferred_element_type=F32)
    for h in hd:
        h["res"] = (eye - h["xinv"]) - _dot_split(h["nmat"], h["xinv"])
    for h in hd:
        h["xinv"] = h["xinv"] + jnp.dot(h["xinv"].astype(BF16), h["res"].astype(BF16), preferred_element_type=F32)
    for h in hd:
        w = _dot_split(h["xinv"], h["rhs"])
        h["val"] = w[:, :HEAD_W]
        h["kcum"] = w[:, HEAD_W:].astype(BF16)

    ss = [s_ref[hh] for hh in range(DN_HEADS)]
    us = [[None] * n_chunks for _ in range(DN_HEADS)]
    oqs = [[None] * n_chunks for _ in range(DN_HEADS)]
    for c in order:
        rs = slice(C * c, C * (c + 1))
        for hh, h in enumerate(hd):
            ks = jnp.dot(jnp.concatenate([h["kcum"][rs], h["qg"][rs]], axis=0), ss[hh].astype(BF16),
                         preferred_element_type=F32)
            u16 = (h["val"][rs] - ks[:C]).astype(BF16)
            oqs[hh][c] = ks[C:]
            us[hh][c] = u16
            ss[hh] = ss[hh] * h["gend"][C * c:C * c + 1, :] + _dot_tn(h["kend"][rs], u16)
    o_heads = []
    for hh, h in enumerate(hd):
        s_ref[hh] = ss[hh]
        sfin_ref[0, hh] = ss[hh]
        o_heads.append(jnp.concatenate(oqs[hh], axis=0)
                       + jnp.dot(h["qk"], jnp.concatenate(us[hh], axis=0), preferred_element_type=F32))

    o = jnp.concatenate(o_heads, axis=1)
    if rev:
        o = o + of_ref[...]
        z = z_ref[...]
        cols = []
        for hh in range(DN_HEADS):
            oh = o[:, HEAD_W * hh:HEAD_W * (hh + 1)]
            cols.append(oh * lax.rsqrt(jnp.mean(oh * oh, axis=-1, keepdims=True) + EPS) * ng_ref[...])
        out_ref[...] = (jnp.concatenate(cols, axis=1) * (z * jax.nn.sigmoid(z))).astype(BF16)
    else:
        out_ref[...] = o


def _dn_dir(rev, proj, of, s0, cw, alog_row, dt_row, norm_g):
    def tile_map(t):
        return (N_SEQ_TILES - 1 - t) if rev else t

    row_blk = lambda cb_: pl.BlockSpec((SEQ_TILE, COL_BLK), lambda t: (tile_map(t), cb_))
    full = lambda shp: pl.BlockSpec(shp, lambda t: (0,) * len(shp))
    in_specs = [row_blk(CB_DN_Q), row_blk(CB_DN_K), row_blk(CB_DN_V),
                pl.BlockSpec((SEQ_TILE, HEAD_W), lambda t: (tile_map(t), COL_BA // HEAD_W))]
    args = [proj, proj, proj, proj]
    if rev:
        in_specs += [pl.BlockSpec((SEQ_TILE, BR_W), lambda t: (tile_map(t), 0)), row_blk(CB_DN_Z)]
        args += [of, proj]
    in_specs += [
        pl.BlockSpec((1, DN_HEADS, HEAD_W, HEAD_W), lambda t: (_tile_seq(tile_map(t)), 0, 0, 0)),
        full((4, 3 * BR_W)), full((1, HEAD_W)), full((1, HEAD_W)),
    ]
    args += [s0, cw, alog_row, dt_row]
    if rev:
        in_specs += [full((1, HEAD_W))]
        args += [norm_g.reshape(1, HEAD_W)]
    return pl.pallas_call(
        functools.partial(_dn_body, rev),
        grid=(N_SEQ_TILES,),
        in_specs=in_specs,
        out_specs=[
            pl.BlockSpec((SEQ_TILE, BR_W), lambda t: (tile_map(t), 0)),
            pl.BlockSpec((1, DN_HEADS, HEAD_W, HEAD_W), lambda t: (jnp.minimum(tile_map(t), CTX_TILES), 0, 0, 0)),
        ],
        out_shape=[
            jax.ShapeDtypeStruct((N_ROWS, BR_W), BF16 if rev else F32),
            jax.ShapeDtypeStruct((CTX_TILES + 1, DN_HEADS, HEAD_W, HEAD_W), F32),
        ],
        scratch_shapes=[pltpu.VMEM((3, HALO, BR_W), F32), pltpu.VMEM((DN_HEADS, HEAD_W, HEAD_W), F32)],
        compiler_params=_cparams(("arbitrary",)),
        name="deltanet_bwd" if rev else "deltanet_fwd",
    )(*args)


MG_TM = 512
MG_TN = 512


def _route_top2(logits):
    idx = lax.broadcasted_iota(jnp.int32, logits.shape, 1)
    m1 = jnp.max(logits, axis=-1, keepdims=True)
    i1 = jnp.min(jnp.where(logits == m1, idx, N_EXPERTS), axis=-1, keepdims=True)
    rest = jnp.where(idx == i1, -jnp.inf, logits)
    m2 = jnp.max(rest, axis=-1, keepdims=True)
    i2 = jnp.min(jnp.where(rest == m2, idx, N_EXPERTS), axis=-1, keepdims=True)
    e2 = jnp.exp(m2 - m1)
    w1 = 1.0 / (1.0 + e2)
    lane = lax.broadcasted_iota(jnp.int32, (logits.shape[0], 4), 1)
    return jnp.where(lane == 0, i1.astype(F32),
                     jnp.where(lane == 1, i2.astype(F32), jnp.where(lane == 2, w1, e2 * w1)))


def _merge_body(moe, *refs):
    (x_ref, mod_ref, ya_ref, yb_ref, yc_ref, yd_ref, g0_ref, g1_ref, g2_ref, g3_ref, wbr_ref, wout_ref,
     n2g_ref) = refs[:13]
    if moe:
        wr_ref, br_ref, o_ref, h_ref, route_ref, acc_ref = refs[13:]
    else:
        o_ref, h_ref, acc_ref = refs[13:]
    j = pl.program_id(1)

    @pl.when(j == 0)
    def _():
        acc_ref[...] = jnp.zeros_like(acc_ref)

    m = None
    for k, (y_ref, g_ref) in enumerate(((ya_ref, g0_ref), (yb_ref, g1_ref), (yc_ref, g2_ref), (yd_ref, g3_ref))):
        br = jnp.dot(y_ref[...], wbr_ref[k], preferred_element_type=F32)
        term = jax.nn.sigmoid(g_ref[...].astype(F32)) * br
        m = term if m is None else m + term
    acc_ref[...] += jnp.dot(m.astype(BF16), wout_ref[...], preferred_element_type=F32)

    @pl.when(j == pl.num_programs(1) - 1)
    def _():
        xn = x_ref[...] + mod_ref[0, 2:3, :] * acc_ref[...]
        o_ref[...] = xn
        h = _rms(xn, n2g_ref[...]) * (1.0 + mod_ref[0, 4:5, :]) + mod_ref[0, 3:4, :]
        h_ref[...] = h.astype(h_ref.dtype)
        if moe:
            route_ref[...] = _route_top2(_dot_split(h, wr_ref[...]) + br_ref[...])


def _merge(x, mod, ys, gates, w_br, w_out, norm2_g, wr=None, br=None):
    moe = wr is not None
    nj = D_MODEL // MG_TN
    y_spec = pl.BlockSpec((MG_TM, BR_W), lambda i, j: (i, 0))
    gate_spec = lambda k: pl.BlockSpec((MG_TM, MG_TN), lambda i, j: (i, k * nj + j))
    row_spec = lambda w: pl.BlockSpec((MG_TM, w), lambda i, j: (i, 0))
    in_specs = [
        row_spec(D_MODEL),
        pl.BlockSpec((1, 6, D_MODEL), lambda i, j: (_row_group(i * MG_TM), 0, 0)),
        y_spec, y_spec, y_spec, y_spec,
        gate_spec(0), gate_spec(1), gate_spec(2), gate_spec(3),
        pl.BlockSpec((4, BR_W, MG_TN), lambda i, j: (0, 0, j)),
        pl.BlockSpec((MG_TN, D_MODEL), lambda i, j: (j, 0)),
        pl.BlockSpec((1, D_MODEL), lambda i, j: (0, 0)),
    ]
    args = [x, mod, *ys, gates, gates, gates, gates, w_br, w_out, norm2_g.reshape(1, D_MODEL)]
    out_specs = [row_spec(D_MODEL), row_spec(D_MODEL)]
    out_shape = [jax.ShapeDtypeStruct((N_ROWS, D_MODEL), F32),
                 jax.ShapeDtypeStruct((N_ROWS, D_MODEL), F32 if moe else BF16)]
    if moe:
        in_specs += [pl.BlockSpec((D_MODEL, N_EXPERTS), lambda i, j: (0, 0)),
                     pl.BlockSpec((1, N_EXPERTS), lambda i, j: (0, 0))]
        args += [wr, br.reshape(1, N_EXPERTS)]
        out_specs += [row_spec(4)]
        out_shape += [jax.ShapeDtypeStruct((N_ROWS, 4), F32)]
    return pl.pallas_call(
        functools.partial(_merge_body, moe),
        grid=(N_ROWS // MG_TM, nj),
        in_specs=in_specs,
        out_specs=out_specs,
        out_shape=out_shape,
        scratch_shapes=[pltpu.VMEM((MG_TM, D_MODEL), F32)],
        compiler_params=_cparams(("arbitrary", "arbitrary")),
        name="merge_out",
    )(*args)


FF_TM = 512
FF_TF = 512


def _ffn_body(final, h_ref, x_ref, mod_ref, fg_ref, w1_ref, w3_ref, w2_ref, o_ref, acc_ref):
    s = pl.program_id(1)

    @pl.when(s == 0)
    def _():
        acc_ref[...] = jnp.zeros_like(acc_ref)

    h = h_ref[...]
    a1 = jnp.dot(h, w1_ref[...], preferred_element_type=F32)
    a3 = jnp.dot(h, w3_ref[...], preferred_element_type=F32)
    act = a1 * jax.nn.sigmoid(a1) * a3
    acc_ref[...] += jnp.dot(act.astype(BF16), w2_ref[...], preferred_element_type=F32)

    @pl.when(s == pl.num_programs(1) - 1)
    def _():
        xn = x_ref[...] + mod_ref[0, 5:6, :] * acc_ref[...]
        o_ref[...] = _rms(xn, fg_ref[...]) if final else xn


def _ffn(h, x, mod, final_g, w1, w3, w2, final):
    ff = w1.shape[1]
    return pl.pallas_call(
        functools.partial(_ffn_body, final),
        grid=(N_ROWS // FF_TM, ff // FF_TF),
        in_specs=[
            pl.BlockSpec((FF_TM, D_MODEL), lambda i, s: (i, 0)),
            pl.BlockSpec((FF_TM, D_MODEL), lambda i, s: (i, 0)),
            pl.BlockSpec((1, 6, D_MODEL), lambda i, s: (_row_group(i * FF_TM), 0, 0)),
            pl.BlockSpec((1, D_MODEL), lambda i, s: (0, 0)),
            pl.BlockSpec((D_MODEL, FF_TF), lambda i, s: (0, s)),
            pl.BlockSpec((D_MODEL, FF_TF), lambda i, s: (0, s)),
            pl.BlockSpec((FF_TF, D_MODEL), lambda i, s: (s, 0)),
        ],
        out_specs=pl.BlockSpec((FF_TM, D_MODEL), lambda i, s: (i, 0)),
        out_shape=jax.ShapeDtypeStruct((N_ROWS, D_MODEL), F32),
        scratch_shapes=[pltpu.VMEM((FF_TM, D_MODEL), F32)],
        compiler_params=_cparams(("arbitrary", "arbitrary")),
        name="dense_ffn",
    )(h, x, mod, final_g.reshape(1, D_MODEL), w1, w3, w2)


TOP_K = 2
N_ASSIGN = TOP_K * N_ROWS
MOE_TM = 512
MOE_TF = 1408
MOE_TILES = 72
MOE_ROWS = MOE_TILES * MOE_TM
CB_TM = 256


def _route_plan(route):
    ids = route[:, :TOP_K].astype(jnp.int32).reshape(-1)
    onehot = (ids[:, None] == jnp.arange(N_EXPERTS, dtype=jnp.int32)[None, :]).astype(jnp.int32)
    csum = jnp.cumsum(onehot, axis=0)
    counts = csum[-1]
    padded = ((counts + MOE_TM - 1) // MOE_TM) * MOE_TM
    end_pad = jnp.cumsum(padded)
    start_pad = end_pad - padded
    pos = jnp.sum(onehot * (csum - 1 + start_pad[None, :]), axis=1).astype(jnp.int32)
    token = jnp.arange(N_ASSIGN, dtype=jnp.int32) // TOP_K
    n_used = (end_pad[-1] // MOE_TM).astype(jnp.int32).reshape(1)
    tile = jnp.arange(MOE_TILES, dtype=jnp.int32)
    tile_expert = jnp.minimum(jnp.sum((tile[:, None] * MOE_TM >= end_pad[None, :]).astype(jnp.int32), axis=1),
                              N_EXPERTS - 1).astype(jnp.int32)
    k_in_expert = tile - start_pad[tile_expert] // MOE_TM
    cnt = jnp.where(tile < n_used[0], jnp.clip(counts[tile_expert] - k_in_expert * MOE_TM, 0, MOE_TM), 0)
    valid = jnp.arange(MOE_TM, dtype=jnp.int32)[None, :] < cnt[:, None]
    rows = jnp.arange(MOE_ROWS, dtype=jnp.int32).reshape(MOE_TILES, MOE_TM)
    free_rows = lax.sort(jnp.where(valid, MOE_ROWS, rows).reshape(-1))[:MOE_ROWS - N_ASSIGN]
    _, row_token = lax.sort((jnp.concatenate([pos, free_rows]),
                             jnp.concatenate([token, jnp.zeros((MOE_ROWS - N_ASSIGN,), jnp.int32)])), num_keys=1)
    return pos, row_token, tile_expert, n_used


def _start_row_gather(idx_ref, n_rows, src_ref, dst_ref, sem):
    def issue(p, c):
        for u in range(4):
            r = 4 * p + u
            pltpu.make_async_copy(src_ref.at[pl.ds(idx_ref[0, 0, r], 1)], dst_ref.at[pl.ds(r, 1)], sem).start(priority=u % 2)
        return c

    lax.fori_loop(0, n_rows // 4, issue, 0)


def _wait_row_gather(n_rows, src_ref, dst_ref, sem):
    pltpu.make_async_copy(src_ref.at[pl.ds(0, n_rows)], dst_ref, sem).wait()


def _moe_up_body(te_ref, nu_ref, tok_ref, tok_next_ref, src_ref, w1_ref, w3_ref, act_ref, gbuf_ref, h_ref, sems):
    i = pl.program_id(0)
    f = pl.program_id(1)
    n_used = nu_ref[0]
    used = i < n_used
    slot = lax.rem(i, 2)

    def start_tile(idx_ref, s):
        _start_row_gather(idx_ref, MOE_TM, src_ref, gbuf_ref.at[s], sems.at[s])

    @pl.when(jnp.logical_and(f == 0, i == 0))
    def _():
        start_tile(tok_ref, 0)

    @pl.when(f == 0)
    def _():
        @pl.when(used)
        def _():
            _wait_row_gather(MOE_TM, src_ref, gbuf_ref.at[slot], sems.at[slot])
            h_ref[...] = gbuf_ref[slot].astype(BF16)

        @pl.when(i + 1 < n_used)
        def _():
            start_tile(tok_next_ref, 1 - slot)

    @pl.when(used)
    def _():
        h = h_ref[...]
        a1 = jnp.dot(h, w1_ref[0], preferred_element_type=F32)
        a3 = jnp.dot(h, w3_ref[0], preferred_element_type=F32)
        act_ref[...] = (a1 * jax.nn.sigmoid(a1) * a3).astype(BF16)

    @pl.when(jnp.logical_not(used))
    def _():
        act_ref[...] = jnp.zeros_like(act_ref)


def _moe_down_body(te_ref, nu_ref, act_ref, w2_ref, o_ref):
    used = pl.program_id(0) < nu_ref[0]

    @pl.when(used)
    def _():
        o_ref[...] = jnp.dot(act_ref[...], w2_ref[0], preferred_element_type=F32)

    @pl.when(jnp.logical_not(used))
    def _():
        o_ref[...] = jnp.zeros_like(o_ref)


def _moe_ffn(tile_expert, n_used, row_token, src, w1, w3, w2):
    ff = w1.shape[2]
    nf = ff // MOE_TF

    def f_blk(i, f, nu):
        return jnp.where(i < nu[0], f, nf - 1)

    tok = row_token.reshape(MOE_TILES, 1, MOE_TM)
    act = pl.pallas_call(
        _moe_up_body,
        grid_spec=pltpu.PrefetchScalarGridSpec(
            num_scalar_prefetch=2,
            grid=(MOE_TILES, nf),
            in_specs=[
                pl.BlockSpec((1, 1, MOE_TM), lambda i, f, te, nu: (i, 0, 0), memory_space=pltpu.SMEM),
                pl.BlockSpec((1, 1, MOE_TM), lambda i, f, te, nu: (jnp.minimum(i + 1, MOE_TILES - 1), 0, 0),
                             memory_space=pltpu.SMEM),
                pl.BlockSpec(memory_space=pl.ANY),
                pl.BlockSpec((1, D_MODEL, MOE_TF), lambda i, f, te, nu: (te[i], 0, f_blk(i, f, nu))),
                pl.BlockSpec((1, D_MODEL, MOE_TF), lambda i, f, te, nu: (te[i], 0, f_blk(i, f, nu))),
            ],
            out_specs=pl.BlockSpec((MOE_TM, MOE_TF), lambda i, f, te, nu: (i, f)),
            scratch_shapes=[pltpu.VMEM((2, MOE_TM, D_MODEL), F32), pltpu.VMEM((MOE_TM, D_MODEL), BF16),
                            pltpu.SemaphoreType.DMA((2,))],
        ),
        out_shape=jax.ShapeDtypeStruct((MOE_ROWS, ff), BF16),
        compiler_params=_cparams(("arbitrary", "arbitrary")),
        name="moe_up",
    )(tile_expert, n_used, tok, tok, src, w1, w3)
    return pl.pallas_call(
        _moe_down_body,
        grid_spec=pltpu.PrefetchScalarGridSpec(
            num_scalar_prefetch=2,
            grid=(MOE_TILES,),
            in_specs=[
                pl.BlockSpec((MOE_TM, ff), lambda i, te, nu: (i, 0)),
                pl.BlockSpec((1, ff, D_MODEL), lambda i, te, nu: (te[i], 0, 0)),
            ],
            out_specs=pl.BlockSpec((MOE_TM, D_MODEL), lambda i, te, nu: (i, 0)),
        ),
        out_shape=jax.ShapeDtypeStruct((MOE_ROWS, D_MODEL), F32),
        compiler_params=_cparams(("arbitrary",)),
        name="moe_down",
    )(tile_expert, n_used, act, w2)


CB_CTX_TILES = N_CTX_ROWS // CB_TM


def _combine_body(final, pos_ref, pos_next_ref, ys_ref, route_ref, x_ref, mod_ref, fg_ref, *refs):
    out_refs, (buf_ref, sems) = refs[:-2], refs[-2:]
    i = pl.program_id(0)
    slot = lax.rem(i, 2)
    n_rows = TOP_K * CB_TM

    @pl.when(i == 0)
    def _():
        _start_row_gather(pos_ref, n_rows, ys_ref, buf_ref.at[0], sems.at[0])

    @pl.when(i + 1 < pl.num_programs(0))
    def _():
        _start_row_gather(pos_next_ref, n_rows, ys_ref, buf_ref.at[1 - slot], sems.at[1 - slot])

    _wait_row_gather(n_rows, ys_ref, buf_ref.at[slot], sems.at[slot])
    route = route_ref[...]
    y = route[:, 2:3] * buf_ref[slot, 0:CB_TM, :] + route[:, 3:4] * buf_ref[slot, CB_TM:, :]
    xn = x_ref[...] + mod_ref[0, 5:6, :] * y
    if final:
        res = _rms(xn, fg_ref[...])

        @pl.when(i < CB_CTX_TILES)
        def _():
            out_refs[0][...] = res

        @pl.when(i >= CB_CTX_TILES)
        def _():
            out_refs[1][...] = res
    else:
        out_refs[0][...] = xn


def _combine(pos, ys, route, x, mod, final_g, final):
    n_tiles = N_ROWS // CB_TM
    pos_t = pos.reshape(n_tiles, CB_TM, TOP_K).transpose(0, 2, 1).reshape(n_tiles, 1, TOP_K * CB_TM)
    if final:
        out_specs = [pl.BlockSpec((CB_TM, D_MODEL), lambda i: (jnp.minimum(i, CB_CTX_TILES - 1), 0)),
                     pl.BlockSpec((CB_TM, D_MODEL), lambda i: (jnp.maximum(i - CB_CTX_TILES, 0), 0))]
        out_shape = [jax.ShapeDtypeStruct((N_CTX_ROWS, D_MODEL), F32),
                     jax.ShapeDtypeStruct((N_ROWS - N_CTX_ROWS, D_MODEL), F32)]
    else:
        out_specs = [pl.BlockSpec((CB_TM, D_MODEL), lambda i: (i, 0))]
        out_shape = [jax.ShapeDtypeStruct((N_ROWS, D_MODEL), F32)]
    return pl.pallas_call(
        functools.partial(_combine_body, final),
        grid=(n_tiles,),
        in_specs=[
            pl.BlockSpec((1, 1, TOP_K * CB_TM), lambda i: (i, 0, 0), memory_space=pltpu.SMEM),
            pl.BlockSpec((1, 1, TOP_K * CB_TM), lambda i: (jnp.minimum(i + 1, n_tiles - 1), 0, 0),
                         memory_space=pltpu.SMEM),
            pl.BlockSpec(memory_space=pl.ANY),
            pl.BlockSpec((CB_TM, 4), lambda i: (i, 0)),
            pl.BlockSpec((CB_TM, D_MODEL), lambda i: (i, 0)),
            pl.BlockSpec((1, 6, D_MODEL), lambda i: (_row_group(i * CB_TM), 0, 0)),
            pl.BlockSpec((1, D_MODEL), lambda i: (0, 0)),
        ],
        out_specs=out_specs,
        out_shape=out_shape,
        scratch_shapes=[pltpu.VMEM((2, TOP_K * CB_TM, D_MODEL), F32), pltpu.SemaphoreType.DMA((2,))],
        compiler_params=_cparams(("arbitrary",)),
        name="moe_combine",
    )(pos_t, pos_t, ys, route, x, mod, final_g.reshape(1, D_MODEL))


def _ba_row(v):
    row = jnp.zeros((1, HEAD_W), F32)
    for d in range(2):
        row = row.at[0, d * 8 + 4:d * 8 + 8].set(v[d])
    return row


def kernel(x_prompt, x_sample, state_rglru, state_delta, c, c_ctx, ada_w, ada_b, norm1_g, norm2_g, w_in, rg_conv_w, rg_conv_b, rg_wa, rg_ba, rg_wx, rg_bx, rg_lam, pool_w, pool_scale, sgu_ln_g, sgu_ln_b, sgu_ws, sgu_bs, dn_conv_w, dn_a_log, dn_dt_bias, dn_norm_g, w_br, w_out, ffn_w1, ffn_w3, ffn_w2, moe_wr, moe_br, moe_w1, moe_w3, moe_w2, final_g):
    x = jnp.concatenate([x_prompt.reshape(N_CTX_ROWS, D_MODEL), x_sample.reshape(N_ROWS - N_CTX_ROWS, D_MODEL)], axis=0)
    c_all = jnp.concatenate([c_ctx[None, :], c, jnp.zeros((8 - 1 - N_LAT_SEQ, D_MODEL), F32)], axis=0)
    mods = _ada(c_all, ada_w, ada_b).reshape(DEPTH, 8, 6, D_MODEL)

    rg_states, dn_states = [], []
    for l in range(DEPTH):
        mod = mods[l]
        w_mix = jnp.pad(w_in[l][:, :COL_BA + 16].astype(BF16), ((0, 0), (0, COL_BLK - 16)))
        w_gate = w_in[l][:, COL_BA + 16:].astype(BF16)
        proj, gates = _inproj(x, mod, norm1_g[l], w_mix, w_gate)

        rg_h0 = [jnp.concatenate([jnp.zeros((N_CTX_SEQ, BR_W), F32), state_rglru[:, l, d]], axis=0)[:, None, :]
                 for d in range(2)]
        rg_args = lambda d: (rg_conv_w[l, d], rg_conv_b[l, d], rg_wa[l, d], rg_ba[l, d], rg_wx[l, d],
                             rg_bx[l, d], rg_lam[l, d])
        hf, st_f = _rglru_dir(False, proj, None, rg_h0[0], *rg_args(0))
        y_a, st_b = _rglru_dir(True, proj, hf, rg_h0[1], *rg_args(1))
        rg_states.append(jnp.stack([st_f[:CTX_TILES, 0], st_b[:CTX_TILES, 0]], axis=1))

        y_b = _pool(proj, pool_w[l], pool_scale[l])
        y_c = _sgu(proj, sgu_ln_g[l], sgu_ln_b[l], sgu_ws[l], sgu_bs[l])

        dn_s0 = [jnp.concatenate([jnp.zeros((N_CTX_SEQ, DN_HEADS, HEAD_W, HEAD_W), F32), state_delta[:, l, d]], axis=0)
                 for d in range(2)]
        alog_row, dt_row = _ba_row(dn_a_log[l]), _ba_row(dn_dt_bias[l])
        of, s_f = _dn_dir(False, proj, None, dn_s0[0], dn_conv_w[l, 0], alog_row, dt_row, None)
        y_d, s_b = _dn_dir(True, proj, of, dn_s0[1], dn_conv_w[l, 1], alog_row, dt_row, dn_norm_g[l])
        dn_states.append(jnp.stack([s_f[:CTX_TILES], s_b[:CTX_TILES]], axis=1))

        j = l // 2
        final = l == DEPTH - 1
        merge_args = (x, mod, (y_a, y_b, y_c, y_d), gates, w_br[l].astype(BF16), w_out[l].astype(BF16), norm2_g[l])
        if l % 2 == 0:
            x, h2 = _merge(*merge_args)
            x = _ffn(h2, x, mod, final_g, ffn_w1[j].astype(BF16), ffn_w3[j].astype(BF16),
                     ffn_w2[j].astype(BF16), final)
        else:
            x, h2, route = _merge(*merge_args, moe_wr[j], moe_br[j])
            pos, row_token, tile_expert, n_used = _route_plan(route)
            ys = _moe_ffn(tile_expert, n_used, row_token, h2, moe_w1[j].astype(BF16), moe_w3[j].astype(BF16),
                          moe_w2[j].astype(BF16))
            outs = _combine(pos, ys, route, x, mod, final_g, final)
            x = outs[0] if not final else None

    if x is None:
        y_prompt, y_sample = outs
    else:
        y_prompt, y_sample = x[:N_CTX_ROWS], x[N_CTX_ROWS:]
    return (y_prompt.reshape(N_CTX_SEQ, CTX_LEN, D_MODEL), y_sample.reshape(N_LAT_SEQ, LAT_LEN, D_MODEL),
            jnp.stack(rg_states, axis=1), jnp.stack(dn_states, axis=1))
```

```python
import functools

import jax
import jax.numpy as jnp
from jax import lax
from jax.experimental import pallas as pl
from jax.experimental.pallas import tpu as pltpu

F32 = jnp.float32
BF16 = jnp.bfloat16

D_MODEL = 2048
DEPTH = 2
N_CTX_SEQ = 32
CTX_LEN = 256
N_LAT_SEQ = 2
LAT_LEN = 4096
GRID_W = 64
N_CTX_ROWS = N_CTX_SEQ * CTX_LEN
N_ROWS = N_CTX_ROWS + N_LAT_SEQ * LAT_LEN
BR_W = 512
RG_C = 8.0
POOL_WINDOWS = (2, 4, 8, 16)
SGU_CHUNK = 128
DN_CHUNK = 64
DN_HEADS = 4
HEAD_W = 128
N_EXPERTS = 8
EPS = 1e-6

SEQ_TILE = 256
N_SEQ_TILES = N_ROWS // SEQ_TILE
CTX_TILES = N_CTX_ROWS // SEQ_TILE
LAT_TILES_PER_SEQ = LAT_LEN // SEQ_TILE
HALO = 8

COL_BLK = 512
CB_RG_X, CB_RG_G, CB_POOL, CB_SGU_U, CB_SGU_V, CB_DN_Q, CB_DN_K, CB_DN_V, CB_DN_Z = range(9)
COL_BA = 4608
CB_GATE0 = 10
P_COLS = 5120 + 4 * D_MODEL

VMEM_LIMIT = 56 * 1024 * 1024


def _cparams(sem, vmem_limit=VMEM_LIMIT):
    return pltpu.CompilerParams(dimension_semantics=sem, vmem_limit_bytes=vmem_limit)


def _row_group(row0):
    return jnp.where(row0 < N_CTX_ROWS, 0, 1 + (row0 - N_CTX_ROWS) // LAT_LEN)


def _tile_seq(tt):
    return jnp.where(tt < CTX_TILES, tt, CTX_TILES + (tt - CTX_TILES) // LAT_TILES_PER_SEQ)


def _neg_expm1_2x(log_a, a):
    x = 2.0 * log_a
    series = -x * (1.0 + x * (0.5 + x * (1.0 / 6.0)))
    return jnp.where(jnp.abs(x) < 0.01, series, 1.0 - a * a)


def _rms(x, g):
    return x * lax.rsqrt(jnp.mean(x * x, axis=-1, keepdims=True) + EPS) * g


def _split16(a):
    ah = a.astype(BF16)
    return ah, (a - ah.astype(F32)).astype(BF16)


def _dot_mask(m16, x):
    n = x.shape[1]
    xh = x.astype(BF16)
    r = x - xh.astype(F32)
    xm = r.astype(BF16)
    xl = (r - xm.astype(F32)).astype(BF16)
    y = jnp.dot(m16, jnp.concatenate([xh, xm, xl], axis=1), preferred_element_type=F32)
    return y[:, :n] + y[:, n:2 * n] + y[:, 2 * n:]


def _ada_body(c_ref, w_ref, b_ref, o_ref):
    c = c_ref[...]
    s = (c * jax.nn.sigmoid(c)).astype(BF16)
    o_ref[0] = jnp.dot(s, w_ref[0].astype(BF16), preferred_element_type=F32) + b_ref[0]


def _ada(c_all, ada_w, ada_b):
    tn = 512
    nj = 6 * D_MODEL // tn
    return pl.pallas_call(
        _ada_body,
        grid=(DEPTH, nj),
        in_specs=[
            pl.BlockSpec((8, D_MODEL), lambda l, j: (0, 0)),
            pl.BlockSpec((1, D_MODEL, tn), lambda l, j: (l, 0, j)),
            pl.BlockSpec((1, 1, tn), lambda l, j: (l, 0, j)),
        ],
        out_specs=pl.BlockSpec((1, 8, tn), lambda l, j: (l, 0, j)),
        out_shape=jax.ShapeDtypeStruct((DEPTH, 8, 6 * D_MODEL), F32),
        compiler_params=_cparams(("arbitrary", "arbitrary")),
        name="ada_mod",
    )(c_all, ada_w, ada_b.reshape(DEPTH, 1, 6 * D_MODEL))


IN_TM = 1024
IN_TN = 1024


IN_NA = (CB_GATE0 * COL_BLK) // IN_TN
IN_VMEM_LIMIT = 60 * 1024 * 1024


def _inproj_body(x_ref, mod_ref, g_ref, wa_ref, wg_ref, om_ref, og_ref, h_ref):
    j = pl.program_id(1)

    @pl.when(j == 0)
    def _():
        y = _rms(x_ref[...], g_ref[...])
        h = y * (1.0 + mod_ref[0, 1:2, :]) + mod_ref[0, 0:1, :]
        h_ref[...] = h.astype(BF16)

    @pl.when(j < IN_NA)
    def _():
        om_ref[...] = jnp.dot(h_ref[...], wa_ref[...], preferred_element_type=F32)

    @pl.when(j >= IN_NA)
    def _():
        og_ref[...] = jnp.dot(h_ref[...], wg_ref[...], preferred_element_type=F32).astype(BF16)


def _inproj(x, mod, norm_g, w_mix, w_gate):
    mix_cols, gate_cols = w_mix.shape[1], w_gate.shape[1]
    return pl.pallas_call(
        _inproj_body,
        grid=(N_ROWS // IN_TM, (mix_cols + gate_cols) // IN_TN),
        in_specs=[
            pl.BlockSpec((IN_TM, D_MODEL), lambda i, j: (i, 0)),
            pl.BlockSpec((1, 6, D_MODEL), lambda i, j: (_row_group(i * IN_TM), 0, 0)),
            pl.BlockSpec((1, D_MODEL), lambda i, j: (0, 0)),
            pl.BlockSpec((D_MODEL, IN_TN), lambda i, j: (0, jnp.minimum(j, IN_NA - 1))),
            pl.BlockSpec((D_MODEL, IN_TN), lambda i, j: (0, jnp.maximum(j - IN_NA, 0))),
        ],
        out_specs=[
            pl.BlockSpec((IN_TM, IN_TN), lambda i, j: (i, jnp.minimum(j, IN_NA - 1))),
            pl.BlockSpec((IN_TM, IN_TN), lambda i, j: (i, jnp.maximum(j - IN_NA, 0))),
        ],
        out_shape=[jax.ShapeDtypeStruct((N_ROWS, mix_cols), F32), jax.ShapeDtypeStruct((N_ROWS, gate_cols), BF16)],
        scratch_shapes=[pltpu.VMEM((IN_TM, D_MODEL), BF16)],
        compiler_params=_cparams(("arbitrary", "arbitrary"), IN_VMEM_LIMIT),
        name="in_proj",
    )(x, mod, norm_g.reshape(1, D_MODEL), w_mix, w_gate)


def _scan_tile(rev, t):
    tt = (N_SEQ_TILES - 1 - t) if rev else t
    edge = (LAT_TILES_PER_SEQ - 1) if rev else 0
    first = jnp.logical_or(tt < CTX_TILES, ((tt - CTX_TILES) % LAT_TILES_PER_SEQ) == edge)
    return tt, first


def _conv_taps(rev, x, halo):
    n = x.shape[0]
    if not rev:
        ext = jnp.concatenate([halo, x], axis=0)
        return lambda k: x if k == 0 else pltpu.roll(ext, k, axis=0)[HALO:]
    ext = jnp.concatenate([x, halo], axis=0)
    return lambda k: x if k == 0 else pltpu.roll(ext, n + HALO - k, axis=0)[:n]


def _short_conv(rev, x, halo, cw):
    tap = _conv_taps(rev, x, halo)
    acc = tap(3) * cw[0:1, :]
    for j in range(1, 4):
        acc = acc + tap(3 - j) * cw[j:j + 1, :]
    return acc


def _next_halo(rev, x):
    return x[0:HALO] if rev else x[x.shape[0] - HALO:]


def _rglru_body(rev, *refs):
    if rev:
        (xr_ref, hf_ref, xg_ref, h0_ref, cw_ref, cb_ref, wa_ref, ba_ref, wx_ref, bx_ref, lam_ref,
         out_ref, st_ref, halo_ref, carry_ref) = refs
    else:
        (xr_ref, h0_ref, cw_ref, cb_ref, wa_ref, ba_ref, wx_ref, bx_ref, lam_ref,
         out_ref, st_ref, halo_ref, carry_ref) = refs
    _, first = _scan_tile(rev, pl.program_id(0))

    @pl.when(first)
    def _():
        halo_ref[...] = jnp.zeros_like(halo_ref)
        carry_ref[...] = h0_ref[0]

    x = xr_ref[...]
    xc = _short_conv(rev, x, halo_ref[...], cw_ref[...]) + cb_ref[...]
    halo_ref[...] = _next_halo(rev, x)

    xcb = xc.astype(BF16)
    rs, gs = [], []
    for hh in range(4):
        xh = xcb[:, HEAD_W * hh:HEAD_W * (hh + 1)]
        rs.append(jnp.dot(xh, wa_ref[hh], preferred_element_type=F32))
        gs.append(jnp.dot(xh, wx_ref[hh], preferred_element_type=F32))
    r = jax.nn.sigmoid(jnp.concatenate(rs, axis=1) + ba_ref[...])
    gi = jax.nn.sigmoid(jnp.concatenate(gs, axis=1) + bx_ref[...])
    log_a = (-RG_C * jax.nn.softplus(-lam_ref[...])) * r
    a = jnp.exp(log_a)
    b = jnp.sqrt(_neg_expm1_2x(log_a, a)) * gi * xc

    n = SEQ_TILE
    row = lax.broadcasted_iota(jnp.int32, (n, BR_W), 0)
    k = 1
    while k < n:
        if rev:
            keep = row < n - k
            a_s = jnp.where(keep, pltpu.roll(a, n - k, axis=0), 1.0)
            b_s = jnp.where(keep, pltpu.roll(b, n - k, axis=0), 0.0)
        else:
            keep = row >= k
            a_s = jnp.where(keep, pltpu.roll(a, k, axis=0), 1.0)
            b_s = jnp.where(keep, pltpu.roll(b, k, axis=0), 0.0)
        b = a * b_s + b
        a = a * a_s
        k *= 2
    h = b + a * carry_ref[...]
    last = h[0:1] if rev else h[n - 1:n]
    carry_ref[...] = last
    st_ref[0] = last
    if rev:
        out_ref[...] = (jax.nn.gelu(xg_ref[...]) * (hf_ref[...] + h)).astype(BF16)
    else:
        out_ref[...] = h


def _rglru_dir(rev, proj, hf, h0, cw, cb, wa, ba, wx, bx, lam):
    def tile_map(t):
        return (N_SEQ_TILES - 1 - t) if rev else t

    row_blk = lambda cb_: pl.BlockSpec((SEQ_TILE, COL_BLK), lambda t: (tile_map(t), cb_))
    full = lambda shp: pl.BlockSpec(shp, lambda t: (0,) * len(shp))
    in_specs = [row_blk(CB_RG_X)]
    args = [proj]
    if rev:
        in_specs += [pl.BlockSpec((SEQ_TILE, BR_W), lambda t: (tile_map(t), 0)), row_blk(CB_RG_G)]
        args += [hf, proj]
    in_specs += [
        pl.BlockSpec((1, 1, BR_W), lambda t: (_tile_seq(tile_map(t)), 0, 0)),
        full((4, BR_W)), full((1, BR_W)), full((4, HEAD_W, HEAD_W)), full((1, BR_W)),
        full((4, HEAD_W, HEAD_W)), full((1, BR_W)), full((1, BR_W)),
    ]
    args += [h0, cw, cb.reshape(1, BR_W), wa.astype(BF16), ba.reshape(1, BR_W), wx.astype(BF16),
             bx.reshape(1, BR_W), lam.reshape(1, BR_W)]
    return pl.pallas_call(
        functools.partial(_rglru_body, rev),
        grid=(N_SEQ_TILES,),
        in_specs=in_specs,
        out_specs=[
            pl.BlockSpec((SEQ_TILE, BR_W), lambda t: (tile_map(t), 0)),
            pl.BlockSpec((1, 1, BR_W), lambda t: (tile_map(t), 0, 0)),
        ],
        out_shape=[
            jax.ShapeDtypeStruct((N_ROWS, BR_W), BF16 if rev else F32),
            jax.ShapeDtypeStruct((N_SEQ_TILES, 1, BR_W), F32),
        ],
        scratch_shapes=[pltpu.VMEM((HALO, BR_W), F32), pltpu.VMEM((1, BR_W), F32)],
        compiler_params=_cparams(("arbitrary",)),
        name="rglru_bwd" if rev else "rglru_fwd",
    )(*args)


def _pool_body(x_ref, w_ref, sc_ref, o_ref):
    shift = jnp.where(pl.program_id(0) < CTX_TILES, CTX_LEN.bit_length() - 1, GRID_W.bit_length() - 1)
    t = lax.broadcasted_iota(jnp.int32, (SEQ_TILE, SEQ_TILE), 0)
    s = lax.broadcasted_iota(jnp.int32, (SEQ_TILE, SEQ_TILE), 1)
    same = jnp.right_shift(t, shift) == jnp.right_shift(s, shift)
    d = s - t
    x = x_ref[...]
    outs = []
    for j, w in enumerate(POOL_WINDOWS):
        inwin = jnp.logical_and(jnp.logical_and(d >= -(w // 2), d <= w - 1 - w // 2), same)
        m = jnp.where(inwin, 1.0, 0.0)
        cnt = jnp.sum(m, axis=1, keepdims=True)
        xg = x[:, HEAD_W * j:HEAD_W * (j + 1)]
        ssum = _dot_mask(m.astype(BF16), xg)
        pooled = ssum / cnt - xg
        outs.append(jnp.dot(pooled.astype(BF16), w_ref[j], preferred_element_type=F32))
    o_ref[...] = (jnp.concatenate(outs, axis=1) * sc_ref[...]).astype(BF16)


def _pool(proj, w_pool, scale):
    return pl.pallas_call(
        _pool_body,
        grid=(N_SEQ_TILES,),
        in_specs=[
            pl.BlockSpec((SEQ_TILE, COL_BLK), lambda t: (t, CB_POOL)),
            pl.BlockSpec((4, HEAD_W, HEAD_W), lambda t: (0, 0, 0)),
            pl.BlockSpec((1, BR_W), lambda t: (0, 0)),
        ],
        out_specs=pl.BlockSpec((SEQ_TILE, BR_W), lambda t: (t, 0)),
        out_shape=jax.ShapeDtypeStruct((N_ROWS, BR_W), BF16),
        compiler_params=_cparams(("arbitrary",)),
        name="pool_mixer",
    )(proj, w_pool.astype(BF16), scale.reshape(1, BR_W))


def _sgu_body(u_ref, v_ref, lg_ref, lb_ref, ws_ref, bst_ref, o_ref):
    u = jax.nn.gelu(u_ref[...])
    v = jax.nn.gelu(v_ref[...])
    mu = jnp.mean(v, axis=-1, keepdims=True)
    var = jnp.mean(jnp.square(v - mu), axis=-1, keepdims=True)
    vb = ((v - mu) * lax.rsqrt(var + EPS) * lg_ref[...] + lb_ref[...]).astype(BF16)
    rows = []
    for cc in range(SEQ_TILE // SGU_CHUNK):
        cols = []
        for hh in range(4):
            blk = vb[SGU_CHUNK * cc:SGU_CHUNK * (cc + 1), HEAD_W * hh:HEAD_W * (hh + 1)]
            cols.append(jnp.dot(ws_ref[hh], blk, preferred_element_type=F32) + bst_ref[:, hh:hh + 1])
        rows.append(jnp.concatenate(cols, axis=1))
    o_ref[...] = (u * jnp.concatenate(rows, axis=0)).astype(BF16)


def _sgu(proj, ln_g, ln_b, ws, bs):
    return pl.pallas_call(
        _sgu_body,
        grid=(N_SEQ_TILES,),
        in_specs=[
            pl.BlockSpec((SEQ_TILE, COL_BLK), lambda t: (t, CB_SGU_U)),
            pl.BlockSpec((SEQ_TILE, COL_BLK), lambda t: (t, CB_SGU_V)),
            pl.BlockSpec((1, BR_W), lambda t: (0, 0)),
            pl.BlockSpec((1, BR_W), lambda t: (0, 0)),
            pl.BlockSpec((4, SGU_CHUNK, SGU_CHUNK), lambda t: (0, 0, 0)),
            pl.BlockSpec((SGU_CHUNK, 4), lambda t: (0, 0)),
        ],
        out_specs=pl.BlockSpec((SEQ_TILE, BR_W), lambda t: (t, 0)),
        out_shape=jax.ShapeDtypeStruct((N_ROWS, BR_W), BF16),
        compiler_params=_cparams(("arbitrary",)),
        name="sgu_mixer",
    )(proj, proj, ln_g.reshape(1, BR_W), ln_b.reshape(1, BR_W), ws.astype(BF16), bs.T)


def _dot_nt(a, b):
    return lax.dot_general(a, b, (((1,), (1,)), ((), ())), preferred_element_type=F32)


def _dot_tn(a, b):
    return lax.dot_general(a, b, (((0,), (0,)), ((), ())), preferred_element_type=F32)


def _dot_split(a, b):
    m = a.shape[0]
    ah, al = _split16(a)
    bh, bl = _split16(b)
    top = jnp.dot(jnp.concatenate([ah, al], axis=0), bh, preferred_element_type=F32)
    return top[:m] + top[m:] + jnp.dot(ah, bl, preferred_element_type=F32)


def _l2norm(x):
    return x * lax.rsqrt(jnp.sum(x * x, axis=-1, keepdims=True) + EPS)


def _dn_body(rev, *refs):
    if rev:
        (q_ref, k_ref, v_ref, ba_ref, of_ref, z_ref, s0_ref, cw_ref, alog_ref, dt_ref, ng_ref,
         out_ref, sfin_ref, halo_ref, s_ref) = refs
    else:
        (q_ref, k_ref, v_ref, ba_ref, s0_ref, cw_ref, alog_ref, dt_ref,
         out_ref, sfin_ref, halo_ref, s_ref) = refs
    _, first = _scan_tile(rev, pl.program_id(0))

    @pl.when(first)
    def _():
        halo_ref[...] = jnp.zeros_like(halo_ref)
        s_ref[...] = s0_ref[0]

    ys = []
    for p, ref in enumerate((q_ref, k_ref, v_ref)):
        x = ref[...]
        y = _short_conv(rev, x, halo_ref[p], cw_ref[:, BR_W * p:BR_W * (p + 1)])
        halo_ref[p] = _next_halo(rev, x)
        ys.append(y * jax.nn.sigmoid(y))
    yq, yk, yv = ys

    ba = ba_ref[...]
    beta_all = jax.nn.sigmoid(ba)
    g_all = -jnp.exp(alog_ref[...]) * jax.nn.softplus(ba + dt_ref[...])
    d_off = 8 if rev else 0

    T, C = SEQ_TILE, DN_CHUNK
    n_chunks = T // C
    ri = lax.broadcasted_iota(jnp.int32, (T, T), 0)
    rj = lax.broadcasted_iota(jnp.int32, (T, T), 1)
    chunk_shift = DN_CHUNK.bit_length() - 1
    same = jnp.right_shift(ri, chunk_shift) == jnp.right_shift(rj, chunk_shift)
    incl = jnp.logical_and(same, (ri <= rj) if rev else (ri >= rj))
    strict = jnp.logical_and(same, (ri < rj) if rev else (ri > rj))
    eye = jnp.where(ri == rj, 1.0, 0.0)
    order = range(n_chunks - 1, -1, -1) if rev else range(n_chunks)

    cum_mat = jnp.concatenate([jnp.where(incl, 1.0, 0.0), jnp.where(same, 1.0, 0.0)], axis=0).astype(BF16)
    gsum = _dot_mask(cum_mat, g_all)
    gcum, gtot = gsum[:T], gsum[T:]
    gcum_t = gcum.T

    hd = []
    for hh in range(DN_HEADS):
        ls = slice(HEAD_W * hh, HEAD_W * (hh + 1))
        col = d_off + 4 + hh
        g_col = gcum[:, col:col + 1]
        g_row = gcum_t[col:col + 1, :]
        g_tot = gtot[:, col:col + 1]
        beta = beta_all[:, d_off + hh:d_off + hh + 1]
        qh = _l2norm(yq[:, ls]) * (HEAD_W ** -0.5)
        kh = _l2norm(yk[:, ls])
        decay = jnp.where(incl, jnp.exp(jnp.minimum(g_col - g_row, 0.0)), 0.0)
        eg = jnp.exp(g_col)
        kb = kh * beta
        kh16 = kh.astype(BF16)
        kq = _dot_nt(jnp.concatenate([kb.astype(BF16), qh.astype(BF16)], axis=0), kh16)
        nmat = jnp.where(strict, kq[:T] * decay, 0.0)
        hd.append(dict(
            nmat=nmat,
            rhs=jnp.concatenate([yv[:, ls] * beta, kb * eg], axis=1),
            qk=(kq[T:] * decay).astype(BF16),
            qg=(qh * eg).astype(BF16),
            kend=(kh * jnp.exp(g_tot - g_col)).astype(BF16),
            gend=jnp.exp(g_tot),
        ))

    for h in hd:
        n16 = h["nmat"].astype(BF16)
        h["xinv"] = eye - h["nmat"]
        h["pw"] = jnp.dot(n16, n16, preferred_element_type=F32).astype(BF16)
    for it in range(5):
        for h in hd:
            if it < 4:
                xp = jnp.dot(jnp.concatenate([h["xinv"].astype(BF16), h["pw"]], axis=0), h["pw"],
                             preferred_element_type=F32)
                h["xinv"] = h["xinv"] + xp[:T]
                h["pw"] = xp[T:].astype(BF16)
            else:
                h["xinv"] = h["xinv"] + jnp.dot(h["xinv"].astype(BF16), h["pw"], preferred_element_type=F32)
    for h in hd:
        h["res"] = (eye - h["xinv"]) - _dot_split(h["nmat"], h["xinv"])
    for h in hd:
        h["xinv"] = h["xinv"] + jnp.dot(h["xinv"].astype(BF16), h["res"].astype(BF16), preferred_element_type=F32)
    for h in hd:
        w = _dot_split(h["xinv"], h["rhs"])
        h["val"] = w[:, :HEAD_W]
        h["kcum"] = w[:, HEAD_W:].astype(BF16)

    ss = [s_ref[hh] for hh in range(DN_HEADS)]
    us = [[None] * n_chunks for _ in range(DN_HEADS)]
    oqs = [[None] * n_chunks for _ in range(DN_HEADS)]
    for c in order:
        rs = slice(C * c, C * (c + 1))
        for hh, h in enumerate(hd):
            ks = jnp.dot(jnp.concatenate([h["kcum"][rs], h["qg"][rs]], axis=0), ss[hh].astype(BF16),
                         preferred_element_type=F32)
            u16 = (h["val"][rs] - ks[:C]).astype(BF16)
            oqs[hh][c] = ks[C:]
            us[hh][c] = u16
            ss[hh] = ss[hh] * h["gend"][C * c:C * c + 1, :] + _dot_tn(h["kend"][rs], u16)
    o_heads = []
    for hh, h in enumerate(hd):
        s_ref[hh] = ss[hh]
        sfin_ref[0, hh] = ss[hh]
        o_heads.append(jnp.concatenate(oqs[hh], axis=0)
                       + jnp.dot(h["qk"], jnp.concatenate(us[hh], axis=0), preferred_element_type=F32))

    o = jnp.concatenate(o_heads, axis=1)
    if rev:
        o = o + of_ref[...]
        z = z_ref[...]
        cols = []
        for hh in range(DN_HEADS):
            oh = o[:, HEAD_W * hh:HEAD_W * (hh + 1)]
            cols.append(oh * lax.rsqrt(jnp.mean(oh * oh, axis=-1, keepdims=True) + EPS) * ng_ref[...])
        out_ref[...] = (jnp.concatenate(cols, axis=1) * (z * jax.nn.sigmoid(z))).astype(BF16)
    else:
        out_ref[...] = o


def _dn_dir(rev, proj, of, s0, cw, alog_row, dt_row, norm_g):
    def tile_map(t):
        return (N_SEQ_TILES - 1 - t) if rev else t

    row_blk = lambda cb_: pl.BlockSpec((SEQ_TILE, COL_BLK), lambda t: (tile_map(t), cb_))
    full = lambda shp: pl.BlockSpec(shp, lambda t: (0,) * len(shp))
    in_specs = [row_blk(CB_DN_Q), row_blk(CB_DN_K), row_blk(CB_DN_V),
                pl.BlockSpec((SEQ_TILE, HEAD_W), lambda t: (tile_map(t), COL_BA // HEAD_W))]
    args = [proj, proj, proj, proj]
    if rev:
        in_specs += [pl.BlockSpec((SEQ_TILE, BR_W), lambda t: (tile_map(t), 0)), row_blk(CB_DN_Z)]
        args += [of, proj]
    in_specs += [
        pl.BlockSpec((1, DN_HEADS, HEAD_W, HEAD_W), lambda t: (_tile_seq(tile_map(t)), 0, 0, 0)),
        full((4, 3 * BR_W)), full((1, HEAD_W)), full((1, HEAD_W)),
    ]
    args += [s0, cw, alog_row, dt_row]
    if rev:
        in_specs += [full((1, HEAD_W))]
        args += [norm_g.reshape(1, HEAD_W)]
    return pl.pallas_call(
        functools.partial(_dn_body, rev),
        grid=(N_SEQ_TILES,),
        in_specs=in_specs,
        out_specs=[
            pl.BlockSpec((SEQ_TILE, BR_W), lambda t: (tile_map(t), 0)),
            pl.BlockSpec((1, DN_HEADS, HEAD_W, HEAD_W), lambda t: (jnp.minimum(tile_map(t), CTX_TILES), 0, 0, 0)),
        ],
        out_shape=[
            jax.ShapeDtypeStruct((N_ROWS, BR_W), BF16 if rev else F32),
            jax.ShapeDtypeStruct((CTX_TILES + 1, DN_HEADS, HEAD_W, HEAD_W), F32),
        ],
        scratch_shapes=[pltpu.VMEM((3, HALO, BR_W), F32), pltpu.VMEM((DN_HEADS, HEAD_W, HEAD_W), F32)],
        compiler_params=_cparams(("arbitrary",)),
        name="deltanet_bwd" if rev else "deltanet_fwd",
    )(*args)


MG_TM = 512
MG_TN = 512


def _route_top2(logits):
    idx = lax.broadcasted_iota(jnp.int32, logits.shape, 1)
    m1 = jnp.max(logits, axis=-1, keepdims=True)
    i1 = jnp.min(jnp.where(logits == m1, idx, N_EXPERTS), axis=-1, keepdims=True)
    rest = jnp.where(idx == i1, -jnp.inf, logits)
    m2 = jnp.max(rest, axis=-1, keepdims=True)
    i2 = jnp.min(jnp.where(rest == m2, idx, N_EXPERTS), axis=-1, keepdims=True)
    e2 = jnp.exp(m2 - m1)
    w1 = 1.0 / (1.0 + e2)
    lane = lax.broadcasted_iota(jnp.int32, (logits.shape[0], 4), 1)
    return jnp.where(lane == 0, i1.astype(F32),
                     jnp.where(lane == 1, i2.astype(F32), jnp.where(lane == 2, w1, e2 * w1)))


def _merge_body(moe, *refs):
    (x_ref, mod_ref, ya_ref, yb_ref, yc_ref, yd_ref, g0_ref, g1_ref, g2_ref, g3_ref, wbr_ref, wout_ref,
     n2g_ref) = refs[:13]
    if moe:
        wr_ref, br_ref, o_ref, h_ref, route_ref, acc_ref = refs[13:]
    else:
        o_ref, h_ref, acc_ref = refs[13:]
    j = pl.program_id(1)

    @pl.when(j == 0)
    def _():
        acc_ref[...] = jnp.zeros_like(acc_ref)

    m = None
    for k, (y_ref, g_ref) in enumerate(((ya_ref, g0_ref), (yb_ref, g1_ref), (yc_ref, g2_ref), (yd_ref, g3_ref))):
        br = jnp.dot(y_ref[...], wbr_ref[k], preferred_element_type=F32)
        term = jax.nn.sigmoid(g_ref[...].astype(F32)) * br
        m = term if m is None else m + term
    acc_ref[...] += jnp.dot(m.astype(BF16), wout_ref[...], preferred_element_type=F32)

    @pl.when(j == pl.num_programs(1) - 1)
    def _():
        xn = x_ref[...] + mod_ref[0, 2:3, :] * acc_ref[...]
        o_ref[...] = xn
        h = _rms(xn, n2g_ref[...]) * (1.0 + mod_ref[0, 4:5, :]) + mod_ref[0, 3:4, :]
        h_ref[...] = h.astype(h_ref.dtype)
        if moe:
            route_ref[...] = _route_top2(_dot_split(h, wr_ref[...]) + br_ref[...])


def _merge(x, mod, ys, gates, w_br, w_out, norm2_g, wr=None, br=None):
    moe = wr is not None
    nj = D_MODEL // MG_TN
    y_spec = pl.BlockSpec((MG_TM, BR_W), lambda i, j: (i, 0))
    gate_spec = lambda k: pl.BlockSpec((MG_TM, MG_TN), lambda i, j: (i, k * nj + j))
    row_spec = lambda w: pl.BlockSpec((MG_TM, w), lambda i, j: (i, 0))
    in_specs = [
        row_spec(D_MODEL),
        pl.BlockSpec((1, 6, D_MODEL), lambda i, j: (_row_group(i * MG_TM), 0, 0)),
        y_spec, y_spec, y_spec, y_spec,
        gate_spec(0), gate_spec(1), gate_spec(2), gate_spec(3),
        pl.BlockSpec((4, BR_W, MG_TN), lambda i, j: (0, 0, j)),
        pl.BlockSpec((MG_TN, D_MODEL), lambda i, j: (j, 0)),
        pl.BlockSpec((1, D_MODEL), lambda i, j: (0, 0)),
    ]
    args = [x, mod, *ys, gates, gates, gates, gates, w_br, w_out, norm2_g.reshape(1, D_MODEL)]
    out_specs = [row_spec(D_MODEL), row_spec(D_MODEL)]
    out_shape = [jax.ShapeDtypeStruct((N_ROWS, D_MODEL), F32),
                 jax.ShapeDtypeStruct((N_ROWS, D_MODEL), F32 if moe else BF16)]
    if moe:
        in_specs += [pl.BlockSpec((D_MODEL, N_EXPERTS), lambda i, j: (0, 0)),
                     pl.BlockSpec((1, N_EXPERTS), lambda i, j: (0, 0))]
        args += [wr, br.reshape(1, N_EXPERTS)]
        out_specs += [row_spec(4)]
        out_shape += [jax.ShapeDtypeStruct((N_ROWS, 4), F32)]
    return pl.pallas_call(
        functools.partial(_merge_body, moe),
        grid=(N_ROWS // MG_TM, nj),
        in_specs=in_specs,
        out_specs=out_specs,
        out_shape=out_shape,
        scratch_shapes=[pltpu.VMEM((MG_TM, D_MODEL), F32)],
        compiler_params=_cparams(("arbitrary", "arbitrary")),
        name="merge_out",
    )(*args)


FF_TM = 512
FF_TF = 512


def _ffn_body(final, h_ref, x_ref, mod_ref, fg_ref, w1_ref, w3_ref, w2_ref, o_ref, acc_ref):
    s = pl.program_id(1)

    @pl.when(s == 0)
    def _():
        acc_ref[...] = jnp.zeros_like(acc_ref)

    h = h_ref[...]
    a1 = jnp.dot(h, w1_ref[...], preferred_element_type=F32)
    a3 = jnp.dot(h, w3_ref[...], preferred_element_type=F32)
    act = a1 * jax.nn.sigmoid(a1) * a3
    acc_ref[...] += jnp.dot(act.astype(BF16), w2_ref[...], preferred_element_type=F32)

    @pl.when(s == pl.num_programs(1) - 1)
    def _():
        xn = x_ref[...] + mod_ref[0, 5:6, :] * acc_ref[...]
        o_ref[...] = _rms(xn, fg_ref[...]) if final else xn


def _ffn(h, x, mod, final_g, w1, w3, w2, final):
    ff = w1.shape[1]
    return pl.pallas_call(
        functools.partial(_ffn_body, final),
        grid=(N_ROWS // FF_TM, ff // FF_TF),
        in_specs=[
            pl.BlockSpec((FF_TM, D_MODEL), lambda i, s: (i, 0)),
            pl.BlockSpec((FF_TM, D_MODEL), lambda i, s: (i, 0)),
            pl.BlockSpec((1, 6, D_MODEL), lambda i, s: (_row_group(i * FF_TM), 0, 0)),
            pl.BlockSpec((1, D_MODEL), lambda i, s: (0, 0)),
            pl.BlockSpec((D_MODEL, FF_TF), lambda i, s: (0, s)),
            pl.BlockSpec((D_MODEL, FF_TF), lambda i, s: (0, s)),
            pl.BlockSpec((FF_TF, D_MODEL), lambda i, s: (s, 0)),
        ],
        out_specs=pl.BlockSpec((FF_TM, D_MODEL), lambda i, s: (i, 0)),
        out_shape=jax.ShapeDtypeStruct((N_ROWS, D_MODEL), F32),
        scratch_shapes=[pltpu.VMEM((FF_TM, D_MODEL), F32)],
        compiler_params=_cparams(("arbitrary", "arbitrary")),
        name="dense_ffn",
    )(h, x, mod, final_g.reshape(1, D_MODEL), w1, w3, w2)


TOP_K = 2
N_ASSIGN = TOP_K * N_ROWS
MOE_TM = 512
MOE_TF = 1408
MOE_TILES = 72
MOE_ROWS = MOE_TILES * MOE_TM
CB_TM = 512
GATHER_UNROLL = 8


def _route_plan(route):
    ids = route[:, :TOP_K].astype(jnp.int32).reshape(-1)
    onehot = (ids[:, None] == jnp.arange(N_EXPERTS, dtype=jnp.int32)[None, :]).astype(jnp.int32)
    csum = jnp.cumsum(onehot, axis=0)
    counts = csum[-1]
    padded = ((counts + MOE_TM - 1) // MOE_TM) * MOE_TM
    end_pad = jnp.cumsum(padded)
    start_pad = end_pad - padded
    pos = jnp.sum(onehot * (csum - 1 + start_pad[None, :]), axis=1).astype(jnp.int32)
    token = jnp.arange(N_ASSIGN, dtype=jnp.int32) // TOP_K
    n_used = (end_pad[-1] // MOE_TM).astype(jnp.int32).reshape(1)
    tile = jnp.arange(MOE_TILES, dtype=jnp.int32)
    tile_expert = jnp.minimum(jnp.sum((tile[:, None] * MOE_TM >= end_pad[None, :]).astype(jnp.int32), axis=1),
                              N_EXPERTS - 1).astype(jnp.int32)
    k_in_expert = tile - start_pad[tile_expert] // MOE_TM
    cnt = jnp.where(tile < n_used[0], jnp.clip(counts[tile_expert] - k_in_expert * MOE_TM, 0, MOE_TM), 0)
    valid = jnp.arange(MOE_TM, dtype=jnp.int32)[None, :] < cnt[:, None]
    rows = jnp.arange(MOE_ROWS, dtype=jnp.int32).reshape(MOE_TILES, MOE_TM)
    free_rows = lax.sort(jnp.where(valid, MOE_ROWS, rows).reshape(-1))[:MOE_ROWS - N_ASSIGN]
    _, row_token = lax.sort((jnp.concatenate([pos, free_rows]),
                             jnp.concatenate([token, jnp.zeros((MOE_ROWS - N_ASSIGN,), jnp.int32)])), num_keys=1)
    return pos, row_token, tile_expert, n_used


def _start_row_gather(idx_ref, n_rows, src_ref, dst_ref, sem):
    def issue(p, c):
        for u in range(GATHER_UNROLL):
            r = GATHER_UNROLL * p + u
            pltpu.make_async_copy(src_ref.at[pl.ds(idx_ref[0, 0, r], 1)], dst_ref.at[pl.ds(r, 1)], sem).start(priority=u % 2)
        return c

    lax.fori_loop(0, n_rows // GATHER_UNROLL, issue, 0)


def _wait_row_gather(n_rows, src_ref, dst_ref, sem):
    pltpu.make_async_copy(src_ref.at[pl.ds(0, n_rows)], dst_ref, sem).wait()


def _moe_up_body(te_ref, nu_ref, tok_ref, tok_next_ref, src_ref, w1_ref, w3_ref, act_ref, gbuf_ref, h_ref, sems):
    i = pl.program_id(0)
    f = pl.program_id(1)
    n_used = nu_ref[0]
    used = i < n_used
    slot = lax.rem(i, 2)

    def start_tile(idx_ref, s):
        _start_row_gather(idx_ref, MOE_TM, src_ref, gbuf_ref.at[s], sems.at[s])

    @pl.when(jnp.logical_and(f == 0, i == 0))
    def _():
        start_tile(tok_ref, 0)

    @pl.when(f == 0)
    def _():
        @pl.when(used)
        def _():
            _wait_row_gather(MOE_TM, src_ref, gbuf_ref.at[slot], sems.at[slot])
            h_ref[...] = gbuf_ref[slot].astype(BF16)

        @pl.when(i + 1 < n_used)
        def _():
            start_tile(tok_next_ref, 1 - slot)

    @pl.when(used)
    def _():
        h = h_ref[...]
        a1 = jnp.dot(h, w1_ref[0], preferred_element_type=F32)
        a3 = jnp.dot(h, w3_ref[0], preferred_element_type=F32)
        act_ref[...] = (a1 * jax.nn.sigmoid(a1) * a3).astype(BF16)

    @pl.when(jnp.logical_not(used))
    def _():
        act_ref[...] = jnp.zeros_like(act_ref)


def _moe_down_body(te_ref, nu_ref, act_ref, w2_ref, o_ref):
    used = pl.program_id(0) < nu_ref[0]

    @pl.when(used)
    def _():
        o_ref[...] = jnp.dot(act_ref[...], w2_ref[0], preferred_element_type=F32)

    @pl.when(jnp.logical_not(used))
    def _():
        o_ref[...] = jnp.zeros_like(o_ref)


def _moe_ffn(tile_expert, n_used, row_token, src, w1, w3, w2):
    ff = w1.shape[2]
    nf = ff // MOE_TF

    def f_blk(i, f, nu):
        return jnp.where(i < nu[0], f, nf - 1)

    tok = row_token.reshape(MOE_TILES, 1, MOE_TM)
    act = pl.pallas_call(
        _moe_up_body,
        grid_spec=pltpu.PrefetchScalarGridSpec(
            num_scalar_prefetch=2,
            grid=(MOE_TILES, nf),
            in_specs=[
                pl.BlockSpec((1, 1, MOE_TM), lambda i, f, te, nu: (i, 0, 0), memory_space=pltpu.SMEM),
                pl.BlockSpec((1, 1, MOE_TM), lambda i, f, te, nu: (jnp.minimum(i + 1, MOE_TILES - 1), 0, 0),
                             memory_space=pltpu.SMEM),
                pl.BlockSpec(memory_space=pl.ANY),
                pl.BlockSpec((1, D_MODEL, MOE_TF), lambda i, f, te, nu: (te[i], 0, f_blk(i, f, nu))),
                pl.BlockSpec((1, D_MODEL, MOE_TF), lambda i, f, te, nu: (te[i], 0, f_blk(i, f, nu))),
            ],
            out_specs=pl.BlockSpec((MOE_TM, MOE_TF), lambda i, f, te, nu: (i, f)),
            scratch_shapes=[pltpu.VMEM((2, MOE_TM, D_MODEL), F32), pltpu.VMEM((MOE_TM, D_MODEL), BF16),
                            pltpu.SemaphoreType.DMA((2,))],
        ),
        out_shape=jax.ShapeDtypeStruct((MOE_ROWS, ff), BF16),
        compiler_params=_cparams(("arbitrary", "arbitrary")),
        name="moe_up",
    )(tile_expert, n_used, tok, tok, src, w1, w3)
    return pl.pallas_call(
        _moe_down_body,
        grid_spec=pltpu.PrefetchScalarGridSpec(
            num_scalar_prefetch=2,
            grid=(MOE_TILES,),
            in_specs=[
                pl.BlockSpec((MOE_TM, ff), lambda i, te, nu: (i, 0)),
                pl.BlockSpec((1, ff, D_MODEL), lambda i, te, nu: (te[i], 0, 0)),
            ],
            out_specs=pl.BlockSpec((MOE_TM, D_MODEL), lambda i, te, nu: (i, 0)),
        ),
        out_shape=jax.ShapeDtypeStruct((MOE_ROWS, D_MODEL), F32),
        compiler_params=_cparams(("arbitrary",)),
        name="moe_down",
    )(tile_expert, n_used, act, w2)


CB_CTX_TILES = N_CTX_ROWS // CB_TM


def _combine_body(final, pos_ref, pos_next_ref, ys_ref, route_ref, x_ref, mod_ref, fg_ref, *refs):
    out_refs, (buf_ref, sems) = refs[:-2], refs[-2:]
    i = pl.program_id(0)
    slot = lax.rem(i, 2)
    n_rows = TOP_K * CB_TM

    @pl.when(i == 0)
    def _():
        _start_row_gather(pos_ref, n_rows, ys_ref, buf_ref.at[0], sems.at[0])

    @pl.when(i + 1 < pl.num_programs(0))
    def _():
        _start_row_gather(pos_next_ref, n_rows, ys_ref, buf_ref.at[1 - slot], sems.at[1 - slot])

    _wait_row_gather(n_rows, ys_ref, buf_ref.at[slot], sems.at[slot])
    route = route_ref[...]
    y = route[:, 2:3] * buf_ref[slot, 0:CB_TM, :] + route[:, 3:4] * buf_ref[slot, CB_TM:, :]
    xn = x_ref[...] + mod_ref[0, 5:6, :] * y
    if final:
        res = _rms(xn, fg_ref[...])

        @pl.when(i < CB_CTX_TILES)
        def _():
            out_refs[0][...] = res

        @pl.when(i >= CB_CTX_TILES)
        def _():
            out_refs[1][...] = res
    else:
        out_refs[0][...] = xn


def _combine(pos, ys, route, x, mod, final_g, final):
    n_tiles = N_ROWS // CB_TM
    pos_t = pos.reshape(n_tiles, CB_TM, TOP_K).transpose(0, 2, 1).reshape(n_tiles, 1, TOP_K * CB_TM)
    if final:
        out_specs = [pl.BlockSpec((CB_TM, D_MODEL), lambda i: (jnp.minimum(i, CB_CTX_TILES - 1), 0)),
                     pl.BlockSpec((CB_TM, D_MODEL), lambda i: (jnp.maximum(i - CB_CTX_TILES, 0), 0))]
        out_shape = [jax.ShapeDtypeStruct((N_CTX_ROWS, D_MODEL), F32),
                     jax.ShapeDtypeStruct((N_ROWS - N_CTX_ROWS, D_MODEL), F32)]
    else:
        out_specs = [pl.BlockSpec((CB_TM, D_MODEL), lambda i: (i, 0))]
        out_shape = [jax.ShapeDtypeStruct((N_ROWS, D_MODEL), F32)]
    return pl.pallas_call(
        functools.partial(_combine_body, final),
        grid=(n_tiles,),
        in_specs=[
            pl.BlockSpec((1, 1, TOP_K * CB_TM), lambda i: (i, 0, 0), memory_space=pltpu.SMEM),
            pl.BlockSpec((1, 1, TOP_K * CB_TM), lambda i: (jnp.minimum(i + 1, n_tiles - 1), 0, 0),
                         memory_space=pltpu.SMEM),
            pl.BlockSpec(memory_space=pl.ANY),
            pl.BlockSpec((CB_TM, 4), lambda i: (i, 0)),
            pl.BlockSpec((CB_TM, D_MODEL), lambda i: (i, 0)),
            pl.BlockSpec((1, 6, D_MODEL), lambda i: (_row_group(i * CB_TM), 0, 0)),
            pl.BlockSpec((1, D_MODEL), lambda i: (0, 0)),
        ],
        out_specs=out_specs,
        out_shape=out_shape,
        scratch_shapes=[pltpu.VMEM((2, TOP_K * CB_TM, D_MODEL), F32), pltpu.SemaphoreType.DMA((2,))],
        compiler_params=_cparams(("arbitrary",)),
        name="moe_combine",
    )(pos_t, pos_t, ys, route, x, mod, final_g.reshape(1, D_MODEL))


def _ba_row(v):
    row = jnp.zeros((1, HEAD_W), F32)
    for d in range(2):
        row = row.at[0, d * 8 + 4:d * 8 + 8].set(v[d])
    return row


def kernel(x_prompt, x_sample, state_rglru, state_delta, c, c_ctx, ada_w, ada_b, norm1_g, norm2_g, w_in, rg_conv_w, rg_conv_b, rg_wa, rg_ba, rg_wx, rg_bx, rg_lam, pool_w, pool_scale, sgu_ln_g, sgu_ln_b, sgu_ws, sgu_bs, dn_conv_w, dn_a_log, dn_dt_bias, dn_norm_g, w_br, w_out, ffn_w1, ffn_w3, ffn_w2, moe_wr, moe_br, moe_w1, moe_w3, moe_w2, final_g):
    x = jnp.concatenate([x_prompt.reshape(N_CTX_ROWS, D_MODEL), x_sample.reshape(N_ROWS - N_CTX_ROWS, D_MODEL)], axis=0)
    c_all = jnp.concatenate([c_ctx[None, :], c, jnp.zeros((8 - 1 - N_LAT_SEQ, D_MODEL), F32)], axis=0)
    mods = _ada(c_all, ada_w, ada_b).reshape(DEPTH, 8, 6, D_MODEL)

    rg_states, dn_states = [], []
    for l in range(DEPTH):
        mod = mods[l]
        w_mix = jnp.pad(w_in[l][:, :COL_BA + 16].astype(BF16), ((0, 0), (0, COL_BLK - 16)))
        w_gate = w_in[l][:, COL_BA + 16:].astype(BF16)
        proj, gates = _inproj(x, mod, norm1_g[l], w_mix, w_gate)

        rg_h0 = [jnp.concatenate([jnp.zeros((N_CTX_SEQ, BR_W), F32), state_rglru[:, l, d]], axis=0)[:, None, :]
                 for d in range(2)]
        rg_args = lambda d: (rg_conv_w[l, d], rg_conv_b[l, d], rg_wa[l, d], rg_ba[l, d], rg_wx[l, d],
                             rg_bx[l, d], rg_lam[l, d])
        hf, st_f = _rglru_dir(False, proj, None, rg_h0[0], *rg_args(0))
        y_a, st_b = _rglru_dir(True, proj, hf, rg_h0[1], *rg_args(1))
        rg_states.append(jnp.stack([st_f[:CTX_TILES, 0], st_b[:CTX_TILES, 0]], axis=1))

        y_b = _pool(proj, pool_w[l], pool_scale[l])
        y_c = _sgu(proj, sgu_ln_g[l], sgu_ln_b[l], sgu_ws[l], sgu_bs[l])

        dn_s0 = [jnp.concatenate([jnp.zeros((N_CTX_SEQ, DN_HEADS, HEAD_W, HEAD_W), F32), state_delta[:, l, d]], axis=0)
                 for d in range(2)]
        alog_row, dt_row = _ba_row(dn_a_log[l]), _ba_row(dn_dt_bias[l])
        of, s_f = _dn_dir(False, proj, None, dn_s0[0], dn_conv_w[l, 0], alog_row, dt_row, None)
        y_d, s_b = _dn_dir(True, proj, of, dn_s0[1], dn_conv_w[l, 1], alog_row, dt_row, dn_norm_g[l])
        dn_states.append(jnp.stack([s_f[:CTX_TILES], s_b[:CTX_TILES]], axis=1))

        j = l // 2
        final = l == DEPTH - 1
        merge_args = (x, mod, (y_a, y_b, y_c, y_d), gates, w_br[l].astype(BF16), w_out[l].astype(BF16), norm2_g[l])
        if l % 2 == 0:
            x, h2 = _merge(*merge_args)
            x = _ffn(h2, x, mod, final_g, ffn_w1[j].astype(BF16), ffn_w3[j].astype(BF16),
                     ffn_w2[j].astype(BF16), final)
        else:
            x, h2, route = _merge(*merge_args, moe_wr[j], moe_br[j])
            pos, row_token, tile_expert, n_used = _route_plan(route)
            ys = _moe_ffn(tile_expert, n_used, row_token, h2, moe_w1[j].astype(BF16), moe_w3[j].astype(BF16),
                          moe_w2[j].astype(BF16))
            outs = _combine(pos, ys, route, x, mod, final_g, final)
            x = outs[0] if not final else None

    if x is None:
        y_prompt, y_sample = outs
    else:
        y_prompt, y_sample = x[:N_CTX_ROWS], x[N_CTX_ROWS:]
    return (y_prompt.reshape(N_CTX_SEQ, CTX_LEN, D_MODEL), y_sample.reshape(N_LAT_SEQ, LAT_LEN, D_MODEL),
            jnp.stack(rg_states, axis=1), jnp.stack(dn_states, axis=1))
```
